```python
import math
import jax, jax.numpy as jnp
from jax import lax
import numpy as np

D_MODEL = 1024
BATCH = 2
SEQ = 8192
DEPTH = 2
DEC_BATCH = 32
DEC_SEQ = 4
PAST_LEN = 16384
PAGE_SIZE = 128

N_HEADS = 8
N_KV_HEADS = 4
REP = N_HEADS // N_KV_HEADS
HEAD_DIM = 64
QK_DIM = 2 * HEAD_DIM
V_DIM = 2 * HEAD_DIM
ATTN_Q = N_HEADS * QK_DIM
ATTN_K = N_KV_HEADS * QK_DIM
ATTN_V = N_KV_HEADS * V_DIM
ATTN_OUT = N_HEADS * V_DIM
BLOCK_Q = 128
CHUNK = 128
GMLP_GROUPS = 4
GMLP_DIM = D_MODEL // 2
GMLP_GROUP_DIM = GMLP_DIM // GMLP_GROUPS
CONV_DIM = D_MODEL // 2
CONV_WIDTH = 3
N_BRANCH = 3
D_FF = 2816
N_EXPERTS = 8
TOP_K = 2
D_FF_EXPERT = 2816
N_DENSE = (DEPTH + 1) // 2
N_MOE = DEPTH // 2
ALPHA = (2 * DEPTH) ** 0.25
BETA = (8 * DEPTH) ** -0.25
LN_EPS = 1e-5
Q_END = ATTN_Q
K_END = Q_END + ATTN_K
V_END = K_END + ATTN_V
U_END = V_END + GMLP_DIM
GV_END = U_END + GMLP_DIM
CB_END = GV_END + CONV_DIM
CC_END = CB_END + CONV_DIM
CH_END = CC_END + CONV_DIM
IN_COLS = CH_END + N_BRANCH * D_MODEL
SPLIT_AT = (Q_END, K_END, V_END, U_END, GV_END, CB_END, CC_END, CH_END)

kernel_name = 'hybrid_gated_diffattn_gmlp_shortconv_step'


def layer_norm(x, g, b):
    xf = x.astype(jnp.float32)
    mu = jnp.mean(xf, axis=-1, keepdims=True)
    var = jnp.mean(jnp.square(xf - mu), axis=-1, keepdims=True)
    return ((xf - mu) * lax.rsqrt(var + LN_EPS) * g + b).astype(x.dtype)


def alibi_slopes():
    h = jnp.arange(1, N_HEADS + 1, dtype=jnp.float32)
    return jnp.exp2(-8.0 * h / N_HEADS).reshape(N_KV_HEADS, REP, 1, 1)


def diff_attend(q, k, v, q_pos, k_pos, lam):
    b, tq = q.shape[0], q.shape[1]
    tk = k.shape[1]
    q = q.reshape(b, tq, N_KV_HEADS, REP, 2, HEAD_DIM)
    k = k.reshape(b, tk, N_KV_HEADS, 2, HEAD_DIM)
    s = jnp.einsum('bqgrmd,bkgmd->bmgrqk', q, k).astype(jnp.float32) * (HEAD_DIM ** -0.5)
    dist = (q_pos[:, None] - k_pos[None, :]).astype(jnp.float32)
    s = jnp.where(dist >= 0, s - alibi_slopes() * dist, -jnp.inf)
    p = jax.nn.softmax(s, axis=-1)
    w = p[:, 0] - lam * p[:, 1]
    o = jnp.einsum('bgrqk,bkgd->bqgrd', w.astype(v.dtype), v)
    return o.reshape(b, tq, N_HEADS, V_DIM)


def prompt_attention(q, k, v, lam):
    b, t = q.shape[0], q.shape[1]
    n_blk = t // BLOCK_Q
    q_blocks = q.reshape(b, n_blk, BLOCK_Q, ATTN_Q).swapaxes(0, 1)
    pos = jnp.arange(t)
    q_pos = pos.reshape(n_blk, BLOCK_Q)
    o = lax.map(lambda a: diff_attend(a[0], k, v, a[1], pos, lam), (q_blocks, q_pos))
    return o.swapaxes(0, 1).reshape(b, t, N_HEADS, V_DIM)


def sample_attention(q, k_new, v_new, past_k, past_v, lam):
    k_all = jnp.concatenate([past_k, k_new], axis=1)
    v_all = jnp.concatenate([past_v, v_new], axis=1)
    tk, tq = k_all.shape[1], q.shape[1]
    k_pos = jnp.arange(tk)
    q_pos = (tk - tq) + jnp.arange(tq)
    return diff_attend(q, k_all, v_all, q_pos, k_pos, lam)


def head_rmsnorm(o, g):
    of = o.astype(jnp.float32)
    of = of * lax.rsqrt(jnp.mean(jnp.square(of), axis=-1, keepdims=True) + LN_EPS) * g
    return of.astype(o.dtype)


def gmlp_spatial_gate(u, gv, ln_g, ln_b, ws, bs):
    b, t = u.shape[0], u.shape[1]
    L = min(t, CHUNK)
    z_u = jax.nn.gelu(u)
    z_v = layer_norm(jax.nn.gelu(gv), ln_g, ln_b)
    v_c = z_v.reshape(b, t // L, L, GMLP_GROUPS, GMLP_GROUP_DIM)
    w_s = jnp.tril(ws[:, :L, :L])
    s = jnp.einsum('gij,bnjgc->bnigc', w_s, v_c) + bs[:, :L].T[None, None, :, :, None]
    return z_u * s.reshape(b, t, GMLP_DIM), z_v


def short_conv(cb, cc, ch, w, prev):
    z = cc * ch
    t = z.shape[1]
    zp = jnp.concatenate([prev, z], axis=1)
    y = w[0] * zp[:, 0:t]
    for j in range(1, CONV_WIDTH):
        y = y + w[j] * zp[:, j:j + t]
    return cb * y, zp[:, t:]


def token_mix(x, attend, conv_prev, lam_init, w_in, subln_g, gln_g, gln_b, ws, bs, conv_w, p_attn, p_gmlp, p_conv, w_o):
    b, t = x.shape[0], x.shape[1]
    q, k, v, u, gv, cb, cc, ch, gt = jnp.split(x @ w_in, SPLIT_AT, axis=-1)
    k = k.reshape(b, t, N_KV_HEADS, QK_DIM)
    v = v.reshape(b, t, N_KV_HEADS, V_DIM)
    o = head_rmsnorm(attend(q, k, v), subln_g) * (1.0 - lam_init)
    y_a = o.reshape(b, t, ATTN_OUT) @ p_attn
    g_out, z_v = gmlp_spatial_gate(u, gv, gln_g, gln_b, ws, bs)
    y_g = g_out @ p_gmlp
    c_out, conv_state = short_conv(cb, cc, ch, conv_w, conv_prev)
    y_c = c_out @ p_conv
    g_a, g_g, g_c = jnp.split(jax.nn.sigmoid(gt), N_BRANCH, axis=-1)
    h = (g_a * y_a + g_g * y_g + g_c * y_c) @ w_o
    return h, k, v, conv_state, z_v


def swiglu(x, wg, wu, wd):
    return (jax.nn.silu(x @ wg) * (x @ wu)) @ wd


def moe_swiglu(x, router, wg, wu, wd):
    logits = (x @ router).astype(jnp.float32)
    top_v, top_i = lax.top_k(logits, TOP_K)
    top_w = jax.nn.softmax(top_v, axis=-1)
    combine = jnp.sum(jax.nn.one_hot(top_i, N_EXPERTS, dtype=jnp.float32) * top_w[..., None], axis=-2).astype(x.dtype)
    out = jnp.zeros_like(x)
    for e in range(N_EXPERTS):
        out = out + combine[..., e:e + 1] * swiglu(x, wg[e], wu[e], wd[e])
    return out


def setup_inputs(seed: int = 0) -> dict:
    key = jax.random.key(seed)
    ks = jax.random.split(key, 32)
    f32 = jnp.float32
    n_pages = PAST_LEN // PAGE_SIZE
    n_used = DEC_BATCH * n_pages
    n_pool = n_used + max(1, n_used // 4)

    def nrm(k, shape, scale):
        return jax.random.normal(k, shape, f32) * scale

    w_in = nrm(ks[0], (DEPTH, D_MODEL, IN_COLS), D_MODEL ** -0.5)
    w_in = w_in.at[:, :, K_END:V_END].multiply(BETA)
    page_table = jax.random.permutation(ks[5], n_pool)[:n_used].reshape(DEC_BATCH, n_pages).astype(jnp.int32)
    return {
        'x_prompt': nrm(ks[1], (BATCH, SEQ, D_MODEL), 1.0),
        'x_sample': nrm(ks[2], (DEC_BATCH, DEC_SEQ, D_MODEL), 1.0),
        'cache_k': nrm(ks[3], (DEPTH, n_pool, PAGE_SIZE, N_KV_HEADS, QK_DIM), 1.0),
        'cache_v': nrm(ks[4], (DEPTH, n_pool, PAGE_SIZE, N_KV_HEADS, V_DIM), 1.0),
        'state_conv': nrm(ks[6], (DEPTH, DEC_BATCH, CONV_WIDTH - 1, CONV_DIM), 1.0),
        'page_table': page_table,
        'w_in': w_in,
        'lam_params': nrm(ks[7], (DEPTH, 4, HEAD_DIM), 0.1),
        'subln_g': 1.0 + nrm(ks[8], (DEPTH, V_DIM), 0.02),
        'gmlp_ln_g': 1.0 + nrm(ks[9], (DEPTH, GMLP_DIM), 0.02),
        'gmlp_ln_b': nrm(ks[10], (DEPTH, GMLP_DIM), 0.02),
        'gmlp_ws': nrm(ks[11], (DEPTH, GMLP_GROUPS, CHUNK, CHUNK), CHUNK ** -0.5),
        'gmlp_bs': 1.0 + nrm(ks[12], (DEPTH, GMLP_GROUPS, CHUNK), 0.02),
        'conv_w': nrm(ks[13], (DEPTH, CONV_WIDTH, CONV_DIM), CONV_WIDTH ** -0.5),
        'p_attn': nrm(ks[14], (DEPTH, ATTN_OUT, D_MODEL), ATTN_OUT ** -0.5),
        'p_gmlp': nrm(ks[15], (DEPTH, GMLP_DIM, D_MODEL), GMLP_DIM ** -0.5),
        'p_conv': nrm(ks[16], (DEPTH, CONV_DIM, D_MODEL), CONV_DIM ** -0.5),
        'w_o': nrm(ks[17], (DEPTH, D_MODEL, D_MODEL), BETA * D_MODEL ** -0.5),
        'ln1_g': 1.0 + nrm(ks[18], (DEPTH, D_MODEL), 0.02),
        'ln1_b': nrm(ks[19], (DEPTH, D_MODEL), 0.02),
        'ln2_g': 1.0 + nrm(ks[20], (DEPTH, D_MODEL), 0.02),
        'ln2_b': nrm(ks[21], (DEPTH, D_MODEL), 0.02),
        'ffn_gate': nrm(ks[22], (N_DENSE, D_MODEL, D_FF), D_MODEL ** -0.5),
        'ffn_up': nrm(ks[23], (N_DENSE, D_MODEL, D_FF), D_MODEL ** -0.5),
        'ffn_down': nrm(ks[24], (N_DENSE, D_FF, D_MODEL), BETA * D_FF ** -0.5),
        'router': nrm(ks[25], (N_MOE, D_MODEL, N_EXPERTS), D_MODEL ** -0.5),
        'moe_gate': nrm(ks[26], (N_MOE, N_EXPERTS, D_MODEL, D_FF_EXPERT), D_MODEL ** -0.5),
        'moe_up': nrm(ks[27], (N_MOE, N_EXPERTS, D_MODEL, D_FF_EXPERT), D_MODEL ** -0.5),
        'moe_down': nrm(ks[28], (N_MOE, N_EXPERTS, D_FF_EXPERT, D_MODEL), BETA * D_FF_EXPERT ** -0.5),
    }


def reference(x_prompt, x_sample, cache_k, cache_v, state_conv, page_table, w_in, lam_params, subln_g, gmlp_ln_g, gmlp_ln_b, gmlp_ws, gmlp_bs, conv_w, p_attn, p_gmlp, p_conv, w_o, ln1_g, ln1_b, ln2_g, ln2_b, ffn_gate, ffn_up, ffn_down, router, moe_gate, moe_up, moe_down):
    n_pages = PAST_LEN // PAGE_SIZE
    yp, ys = x_prompt, x_sample
    kp, vp, cp, k_s, v_s, c_s, g_s = [], [], [], [], [], [], []
    for l in range(DEPTH):
        lam_init = 0.8 - 0.6 * math.exp(-0.3 * l)
        lp = lam_params[l].astype(jnp.float32)
        lam = jnp.exp(jnp.sum(lp[0] * lp[1])) - jnp.exp(jnp.sum(lp[2] * lp[3])) + lam_init

        def layer(x, attend, conv_prev):
            h, k, v, conv_state, z_v = token_mix(x, attend, conv_prev, lam_init, w_in[l], subln_g[l], gmlp_ln_g[l], gmlp_ln_b[l], gmlp_ws[l], gmlp_bs[l], conv_w[l], p_attn[l], p_gmlp[l], p_conv[l], w_o[l])
            x = layer_norm(ALPHA * x + h, ln1_g[l], ln1_b[l])
            if l % 2 == 0:
                f = swiglu(x, ffn_gate[l // 2], ffn_up[l // 2], ffn_down[l // 2])
            else:
                f = moe_swiglu(x, router[l // 2], moe_gate[l // 2], moe_up[l // 2], moe_down[l // 2])
            x = layer_norm(ALPHA * x + f, ln2_g[l], ln2_b[l])
            return x, k, v, conv_state, z_v

        conv0 = jnp.zeros((x_prompt.shape[0], CONV_WIDTH - 1, CONV_DIM), x_prompt.dtype)
        yp, k1, v1, c1, _ = layer(yp, lambda q, k, v: prompt_attention(q, k, v, lam), conv0)
        kp.append(k1)
        vp.append(v1)
        cp.append(c1)

        past_k = cache_k[l, page_table].reshape(DEC_BATCH, n_pages * PAGE_SIZE, N_KV_HEADS, QK_DIM)
        past_v = cache_v[l, page_table].reshape(DEC_BATCH, n_pages * PAGE_SIZE, N_KV_HEADS, V_DIM)
        ys, k2, v2, c2, z2 = layer(ys, lambda q, k, v: sample_attention(q, k, v, past_k, past_v, lam), state_conv[l])
        k_s.append(k2)
        v_s.append(v2)
        c_s.append(c2)
        g_s.append(z2)
    return (yp, ys, jnp.stack(kp), jnp.stack(vp), jnp.stack(cp), jnp.stack(k_s), jnp.stack(v_s), jnp.stack(c_s), jnp.stack(g_s))
```

```python
import functools
import math

import jax
import jax.numpy as jnp
from jax import lax
from jax.experimental import pallas as pl
from jax.experimental.pallas import tpu as pltpu

F32 = jnp.float32
BF16 = jnp.bfloat16

N_HEADS = 8
N_KV_HEADS = 4
REP = N_HEADS // N_KV_HEADS
HEAD_DIM = 64
QK_DIM = 2 * HEAD_DIM
V_DIM = 2 * HEAD_DIM
GMLP_GROUPS = 4
CHUNK = 128
CONV_WIDTH = 3
N_BRANCH = 3
LN_EPS = 1e-5
LOG2E = 1.4426950408889634
NEG_BIG = -1e30

LANES = 128
SUBLANES = 8
VMEM_LIMIT = 56 * 1024 * 1024

ATTN_BLOCK = 256
PAGES_PER_STEP = 8
NEW_ROWS = 16


def _cparams(sem):
    return pltpu.CompilerParams(dimension_semantics=sem, vmem_limit_bytes=VMEM_LIMIT)


def _resident(shape):
    nd = len(shape)
    return pl.BlockSpec(shape, lambda *_: (0,) * nd, pipeline_mode=pl.Buffered(1))


def _layer_norm(v, g, b):
    mu = jnp.mean(v, axis=-1, keepdims=True)
    d = v - mu
    var = jnp.mean(d * d, axis=-1, keepdims=True)
    return d * lax.rsqrt(var + LN_EPS) * g + b


def _dot(a, b):
    return jnp.dot(a, b, preferred_element_type=F32)


def _qkv_kernel(x_ref, w_ref, *out_refs, prompt, tk, q_scale):
    d_q = N_HEADS * QK_DIM
    d_k = N_KV_HEADS * QK_DIM
    xb = x_ref[...].astype(BF16)
    q = _dot(xb, w_ref[:, 0:d_q]) * q_scale
    k = _dot(xb, w_ref[:, d_q:d_q + d_k])
    v = _dot(xb, w_ref[:, d_q + d_k:])
    if prompt:
        q0_ref, q1_ref, kf_ref, vf_ref, kb_ref, vt_ref = out_refs
        first_half = (lax.broadcasted_iota(jnp.int32, q.shape, 1) % QK_DIM) < HEAD_DIM
        q0_ref[...] = jnp.where(first_half, q, 0.0).astype(BF16)
        q1_ref[...] = jnp.where(first_half, 0.0, q).astype(BF16)
        kf_ref[...] = k
        vf_ref[...] = v
        kb_ref[...] = k.astype(BF16)
        for g in range(N_KV_HEADS):
            for j in range(v.shape[0] // tk):
                blk = v[j * tk:(j + 1) * tk, g * V_DIM:(g + 1) * V_DIM]
                vt_ref[g, j] = blk.T.astype(BF16)
    else:
        q_ref, kf_ref, vf_ref = out_refs
        q_ref[...] = q
        kf_ref[...] = k
        vf_ref[...] = v


def _qkv_proj(x, w_qkv, *, prompt, tm, seq_len, tk, q_scale):
    n, d = x.shape
    d_q = N_HEADS * QK_DIM
    d_kv = N_KV_HEADS * QK_DIM
    nt = n // tm
    row = lambda i: (i, 0)
    in_specs = [pl.BlockSpec((tm, d), row), _resident(w_qkv.shape)]
    if prompt:
        tps = seq_len // tm
        nb = n // seq_len
        out_shape = [
            jax.ShapeDtypeStruct((n, d_q), BF16), jax.ShapeDtypeStruct((n, d_q), BF16),
            jax.ShapeDtypeStruct((n, d_kv), F32), jax.ShapeDtypeStruct((n, d_kv), F32),
            jax.ShapeDtypeStruct((n, d_kv), BF16),
            jax.ShapeDtypeStruct((nb, N_KV_HEADS, seq_len // tk, V_DIM, tk), BF16),
        ]
        out_specs = [
            pl.BlockSpec((tm, d_q), row), pl.BlockSpec((tm, d_q), row),
            pl.BlockSpec((tm, d_kv), row), pl.BlockSpec((tm, d_kv), row),
            pl.BlockSpec((tm, d_kv), row),
            pl.BlockSpec((None, N_KV_HEADS, tm // tk, V_DIM, tk),
                         lambda i: (i // tps, 0, i % tps, 0, 0)),
        ]
    else:
        out_shape = [jax.ShapeDtypeStruct((n, d_q), F32),
                     jax.ShapeDtypeStruct((n, d_kv), F32), jax.ShapeDtypeStruct((n, d_kv), F32)]
        out_specs = [pl.BlockSpec((tm, d_q), row), pl.BlockSpec((tm, d_kv), row),
                     pl.BlockSpec((tm, d_kv), row)]
    return pl.pallas_call(
        functools.partial(_qkv_kernel, prompt=prompt, tk=tk, q_scale=q_scale),
        grid=(nt,), in_specs=in_specs, out_specs=out_specs, out_shape=out_shape,
        compiler_params=_cparams(("arbitrary",)), name="qkv_proj",
    )(x, w_qkv)


def _prompt_attn_kernel(lam_ref, sig_ref, q0_ref, q1_ref, k_ref, vt_ref, boff_ref, bdiag_ref,
                        gain_ref, o_ref, acc_ref, *, tq, tk, out_scale):
    g = pl.program_id(1)
    qi = pl.program_id(2)
    n_state = REP * 2

    def block(j, bias_ref, carry):
        m_all, l_all = carry
        kj = k_ref[pl.ds(pl.multiple_of(j * tk, tk), tk), :]
        vtj = vt_ref[j]
        dist_blocks = (qi - j).astype(F32) * float(tq)
        m_out, l_out = [], []
        for r in range(REP):
            c_j = -sig_ref[g * REP + r] * dist_blocks
            bias = bias_ref[r]
            for m in range(2):
                h = r * 2 + m
                q_src = q0_ref if m == 0 else q1_ref
                qh = q_src[:, r * QK_DIM:(r + 1) * QK_DIM]
                s_t = lax.dot_general(kj, qh, (((1,), (1,)), ((), ())),
                                      preferred_element_type=F32) + bias
                m_new = jnp.maximum(m_all[h], jnp.max(s_t, axis=0, keepdims=True) + c_j)
                alpha = jnp.exp2(m_all[h] - m_new)
                p_t = jnp.exp2(s_t - (m_new - c_j))
                l_out.append(alpha * l_all[h] + jnp.sum(p_t, axis=0, keepdims=True))
                m_out.append(m_new)
                acc_ref[h] = acc_ref[h] * alpha + _dot(vtj, p_t.astype(BF16))
        return tuple(m_out), tuple(l_out)

    acc_ref[...] = jnp.zeros_like(acc_ref)
    init = (tuple(jnp.full((1, tq), NEG_BIG, F32) for _ in range(n_state)),
            tuple(jnp.zeros((1, tq), F32) for _ in range(n_state)))
    carry = lax.fori_loop(0, qi, lambda j, c: block(j, boff_ref, c), init)
    _, l_all = block(qi, bdiag_ref, carry)

    lam = lam_ref[0]
    gain = gain_ref[...] * out_scale
    for r in range(REP):
        o_t = acc_ref[2 * r] / l_all[2 * r] - lam * (acc_ref[2 * r + 1] / l_all[2 * r + 1])
        ms = jnp.mean(o_t * o_t, axis=0, keepdims=True)
        o_t = o_t * lax.rsqrt(ms + LN_EPS) * gain
        o_ref[:, r * V_DIM:(r + 1) * V_DIM] = o_t.T.astype(o_ref.dtype)


def _prompt_attention(lam, sigma, q0, q1, kb, vt, bias_off, bias_diag, gain, *, seq_len, out_scale):
    n = q0.shape[0]
    nb = n // seq_len
    tq = tk = ATTN_BLOCK
    nq = seq_len // tq
    smem = pl.BlockSpec(memory_space=pltpu.SMEM)
    qspec = pl.BlockSpec((tq, REP * QK_DIM), lambda b, g, i: (b * nq + i, g))
    return pl.pallas_call(
        functools.partial(_prompt_attn_kernel, tq=tq, tk=tk, out_scale=out_scale),
        grid=(nb, N_KV_HEADS, nq),
        in_specs=[
            smem, smem, qspec, qspec,
            pl.BlockSpec((seq_len, QK_DIM), lambda b, g, i: (b, g)),
            pl.BlockSpec((None, None, seq_len // tk, V_DIM, tk), lambda b, g, i: (b, g, 0, 0, 0)),
            pl.BlockSpec((None, REP, tk, tq), lambda b, g, i: (g, 0, 0, 0)),
            pl.BlockSpec((None, REP, tk, tq), lambda b, g, i: (g, 0, 0, 0)),
            pl.BlockSpec((V_DIM, 1), lambda b, g, i: (0, 0)),
        ],
        out_specs=pl.BlockSpec((tq, REP * V_DIM), lambda b, g, i: (b * nq + i, g)),
        out_shape=jax.ShapeDtypeStruct((n, N_HEADS * V_DIM), BF16),
        scratch_shapes=[pltpu.VMEM((REP * 2, V_DIM, tq), F32)],
        compiler_params=_cparams(("arbitrary", "arbitrary", "arbitrary")), name="prompt_attn",
    )(lam, sigma, q0, q1, kb, vt, bias_off, bias_diag, gain)


def _sample_attn_kernel(pt_ref, lam_ref, qbd_ref, *refs, n_pages_step, out_scale):
    del pt_ref
    P = n_pages_step
    k_refs = refs[:P]
    v_refs = refs[P:2 * P]
    (base_ref, pagevec_ref, knew_ref, vnew_ref, bnew_ref, gain_ref,
     o_ref, m_ref, l_ref, acc_ref) = refs[2 * P:]
    c = pl.program_id(1)
    n_c = pl.num_programs(1)
    rows_g = o_ref.shape[1]
    half = N_KV_HEADS * rows_g

    @pl.when(c == 0)
    def _():
        m_ref[...] = jnp.full_like(m_ref, NEG_BIG)
        l_ref[...] = jnp.zeros_like(l_ref)
        acc_ref[...] = jnp.zeros_like(acc_ref)

    qbd = qbd_ref[...]

    def diag_blocks(pv):
        parts = []
        for m in range(2):
            for g in range(N_KV_HEADS):
                r0 = m * half + g * rows_g
                parts.append(pv[r0:r0 + rows_g, g * V_DIM:(g + 1) * V_DIM])
        return jnp.concatenate(parts, axis=0)

    def col_bcast(row):
        return jnp.transpose(jnp.broadcast_to(row, (LANES, LANES)))

    def update(s_tiles, v_tiles):
        m_old = m_ref[...]
        mx = s_tiles[0].max(axis=0, keepdims=True)
        for s_t in s_tiles[1:]:
            mx = jnp.maximum(mx, s_t.max(axis=0, keepdims=True))
        m_new = jnp.maximum(m_old, mx)
        alpha = jnp.exp2(m_old - m_new)
        lsum = jnp.zeros_like(m_old)
        pv = None
        for s_t, v_t in zip(s_tiles, v_tiles):
            p_t = jnp.exp2(s_t - m_new)
            lsum = lsum + jnp.sum(p_t, axis=0, keepdims=True)
            part = lax.dot_general(p_t.astype(BF16), v_t, (((0,), (0,)), ((), ())),
                                   preferred_element_type=F32)
            pv = part if pv is None else pv + part
        m_ref[...] = m_new
        l_ref[...] = alpha * l_ref[...] + lsum
        acc_ref[...] = acc_ref[...] * col_bcast(alpha)[:2 * half] + diag_blocks(pv)

    base = base_ref[...]
    page0 = (c * P).astype(F32)
    s_tiles, v_tiles = [], []
    for p in range(P):
        s_t = _dot(k_refs[p][...].astype(BF16), qbd) + base + pagevec_ref[...] * (page0 + float(p))
        s_tiles.append(s_t)
        v_tiles.append(v_refs[p][...].astype(BF16))
    update(s_tiles, v_tiles)

    @pl.when(c == n_c - 1)
    def _():
        s_new = _dot(knew_ref[...].astype(BF16), qbd) + bnew_ref[...]
        update([s_new], [vnew_ref[...].astype(BF16)])
        l_col = col_bcast(l_ref[...])
        acc = acc_ref[...]
        o = acc[:half] / l_col[:half] - lam_ref[0] * (acc[half:2 * half] / l_col[half:2 * half])
        ms = jnp.mean(o * o, axis=-1, keepdims=True)
        o = o * lax.rsqrt(ms + LN_EPS) * (gain_ref[...] * out_scale)
        for g in range(N_KV_HEADS):
            o_ref[g] = o[g * rows_g:(g + 1) * rows_g].astype(o_ref.dtype)


def _sample_attention(page_table, lam, qbd, cache_k, cache_v, layer, base, pagevec, knew, vnew,
                      bias_new, gain, *, seq_s, out_scale):
    nb, n_pages = page_table.shape
    P = PAGES_PER_STEP
    page_size, kv_width = cache_k.shape[2], cache_k.shape[3]
    rows_g = REP * seq_s
    smem = pl.BlockSpec(memory_space=pltpu.SMEM)

    def page_spec(p):
        return pl.BlockSpec((None, None, page_size, kv_width),
                            lambda b, c, pt: (layer, pt[b, c * P + p], 0, 0))

    const2 = lambda b, c, pt: (0, 0)
    per_seq = lambda b, c, pt: (b, 0, 0)
    grid_spec = pltpu.PrefetchScalarGridSpec(
        num_scalar_prefetch=1,
        grid=(nb, n_pages // P),
        in_specs=[smem, pl.BlockSpec((None,) + qbd.shape[1:], per_seq)]
        + [page_spec(p) for p in range(P)] + [page_spec(p) for p in range(P)]
        + [pl.BlockSpec(base.shape, const2), pl.BlockSpec(pagevec.shape, const2),
           pl.BlockSpec((None,) + knew.shape[1:], per_seq),
           pl.BlockSpec((None,) + vnew.shape[1:], per_seq),
           pl.BlockSpec(bias_new.shape, const2), pl.BlockSpec(gain.shape, const2)],
        out_specs=pl.BlockSpec((None, N_KV_HEADS, rows_g, V_DIM), lambda b, c, pt: (b, 0, 0, 0)),
        scratch_shapes=[pltpu.VMEM((1, LANES), F32), pltpu.VMEM((1, LANES), F32),
                        pltpu.VMEM((2 * N_KV_HEADS * rows_g, V_DIM), F32)],
    )
    return pl.pallas_call(
        functools.partial(_sample_attn_kernel, n_pages_step=P, out_scale=out_scale),
        grid_spec=grid_spec,
        out_shape=jax.ShapeDtypeStruct((nb, N_KV_HEADS, rows_g, V_DIM), BF16),
        compiler_params=_cparams(("arbitrary", "arbitrary")), name="sample_attn",
    )(page_table, lam, qbd, *([cache_k] * P), *([cache_v] * P), base, pagevec, knew, vnew,
      bias_new, gain)


def _branch_kernel(x_ref, w_ref, lng_ref, lnb_ref, ws_ref, bs_ref, cw_ref, pg_ref, pc_ref, *refs,
                   prompt, tiles_per_seq, period):
    d_model = x_ref.shape[1]
    gd = pg_ref.shape[0]
    cd = pc_ref.shape[0]
    tm = x_ref.shape[0]
    if prompt:
        part_ref, ga_ref, ztail_ref, carry_ref = refs
    else:
        fix1_ref, fix2_ref, part_ref, ga_ref, z_ref, zv_ref = refs
    xb = x_ref[...].astype(BF16)

    def proj(lo, width):
        return _dot(xb, w_ref[:, lo:lo + width])

    z_u = jax.nn.gelu(proj(0, gd))
    z_v = _layer_norm(jax.nn.gelu(proj(gd, gd)), lng_ref[...], lnb_ref[...])
    if not prompt:
        zv_ref[...] = z_v
    gw = gd // GMLP_GROUPS
    z_vb = z_v.astype(BF16)
    rows = []
    for n in range(tm // CHUNK):
        cols = []
        for g in range(GMLP_GROUPS):
            blk = z_vb[n * CHUNK:(n + 1) * CHUNK, g * gw:(g + 1) * gw]
            cols.append(_dot(ws_ref[g], blk) + bs_ref[g])
        rows.append(jnp.concatenate(cols, axis=1))
    s = jnp.concatenate(rows, axis=0)
    y_g = _dot((z_u * s).astype(BF16), pg_ref[...])

    off = 2 * gd
    c_b = proj(off, cd)
    z = proj(off + cd, cd) * proj(off + 2 * cd, cd)
    zr1 = pltpu.roll(z, 1, 0)
    zr2 = pltpu.roll(z, 2, 0)
    if prompt:
        @pl.when(pl.program_id(0) % tiles_per_seq == 0)
        def _():
            carry_ref[...] = jnp.zeros_like(carry_ref)

        prev = carry_ref[...]
        r8 = lax.broadcasted_iota(jnp.int32, (SUBLANES, cd), 0)
        top1 = jnp.where(r8 < 1, pltpu.roll(prev, 1, 0), zr1[:SUBLANES])
        top2 = jnp.where(r8 < 2, pltpu.roll(prev, 2, 0), zr2[:SUBLANES])
        zm1 = jnp.concatenate([top1, zr1[SUBLANES:]], axis=0)
        zm2 = jnp.concatenate([top2, zr2[SUBLANES:]], axis=0)
        tail = z[tm - SUBLANES:]
        carry_ref[...] = tail
        ztail_ref[...] = tail
    else:
        pos = lax.broadcasted_iota(jnp.int32, z.shape, 0) % period
        zm1 = jnp.where(pos >= 1, zr1, fix1_ref[...])
        zm2 = jnp.where(pos >= 2, zr2, fix2_ref[...])
        z_ref[...] = z
    cw = cw_ref[...]
    y = cw[0:1] * zm2 + cw[1:2] * zm1 + cw[2:3] * z
    y_c = _dot((c_b * y).astype(BF16), pc_ref[...])

    off = 2 * gd + 3 * cd
    ga_ref[...] = jax.nn.sigmoid(proj(off, d_model)).astype(ga_ref.dtype)
    part_ref[...] = (jax.nn.sigmoid(proj(off + d_model, d_model)) * y_g
                     + jax.nn.sigmoid(proj(off + 2 * d_model, d_model)) * y_c)


def _branch_proj(x, w_br, ln_g, ln_b, ws_mat, bs_mat, conv_w, p_gmlp, p_conv, fixes, *,
                 prompt, tm, seq_len, period):
    n, d = x.shape
    gd, cd = p_gmlp.shape[0], p_conv.shape[0]
    nt = n // tm
    row = lambda i: (i, 0)
    in_specs = [pl.BlockSpec((tm, d), row), _resident(w_br.shape), _resident(ln_g.shape),
                _resident(ln_b.shape), _resident(ws_mat.shape), _resident(bs_mat.shape),
                _resident(conv_w.shape), _resident(p_gmlp.shape), _resident(p_conv.shape)]
    args = [x, w_br, ln_g, ln_b, ws_mat, bs_mat, conv_w, p_gmlp, p_conv]
    out_shape = [jax.ShapeDtypeStruct((n, d), F32), jax.ShapeDtypeStruct((n, d), BF16)]
    out_specs = [pl.BlockSpec((tm, d), row), pl.BlockSpec((tm, d), row)]
    scratch = []
    if prompt:
        out_shape.append(jax.ShapeDtypeStruct((nt * SUBLANES, cd), F32))
        out_specs.append(pl.BlockSpec((SUBLANES, cd), row))
        scratch.append(pltpu.VMEM((SUBLANES, cd), F32))
    else:
        in_specs += [pl.BlockSpec((tm, cd), row), pl.BlockSpec((tm, cd), row)]
        args += list(fixes)
        out_shape += [jax.ShapeDtypeStruct((n, cd), F32), jax.ShapeDtypeStruct((n, gd), F32)]
        out_specs += [pl.BlockSpec((tm, cd), row), pl.BlockSpec((tm, gd), row)]
    return pl.pallas_call(
        functools.partial(_branch_kernel, prompt=prompt,
                          tiles_per_seq=max(seq_len // tm, 1), period=period),
        grid=(nt,), in_specs=in_specs, out_specs=out_specs, out_shape=out_shape,
        scratch_shapes=scratch,
        compiler_params=_cparams(("arbitrary",)), name="branch_proj",
    )(*args)


def _mix_kernel(x_ref, o_ref, ga_ref, part_ref, pa_ref, wo_ref, g_ref, b_ref, out_ref, *, alpha):
    y_a = _dot(o_ref[...], pa_ref[...])
    mix = ga_ref[...].astype(F32) * y_a + part_ref[...]
    h = _dot(mix.astype(BF16), wo_ref[...])
    out_ref[...] = _layer_norm(alpha * x_ref[...] + h, g_ref[...], b_ref[...])


def _mix_proj(x, o, ga, part, p_attn, w_o, ln_g, ln_b, *, tm, alpha):
    n, d = x.shape
    row = lambda i: (i, 0)
    return pl.pallas_call(
        functools.partial(_mix_kernel, alpha=alpha),
        grid=(n // tm,),
        in_specs=[pl.BlockSpec((tm, d), row), pl.BlockSpec((tm, o.shape[1]), row),
                  pl.BlockSpec((tm, d), row), pl.BlockSpec((tm, d), row),
                  _resident(p_attn.shape), _resident(w_o.shape),
                  _resident(ln_g.shape), _resident(ln_b.shape)],
        out_specs=pl.BlockSpec((tm, d), row),
        out_shape=jax.ShapeDtypeStruct((n, d), F32),
        compiler_params=_cparams(("arbitrary",)), name="mix_proj",
    )(x, o, ga, part, p_attn, w_o, ln_g, ln_b)


def _ffn_kernel(x_ref, wg_ref, wu_ref, wd_ref, g_ref, b_ref, out_ref, *, alpha, ff_chunk):
    x = x_ref[...]
    xb = x.astype(BF16)
    f = None
    for c0 in range(0, wg_ref.shape[1], ff_chunk):
        h = jax.nn.silu(_dot(xb, wg_ref[:, c0:c0 + ff_chunk])) * _dot(xb, wu_ref[:, c0:c0 + ff_chunk])
        part = _dot(h.astype(BF16), wd_ref[c0:c0 + ff_chunk, :])
        f = part if f is None else f + part
    out_ref[...] = _layer_norm(alpha * x + f, g_ref[...], b_ref[...])


def _ff_chunk(d_ff):
    for c in (1408, 1024, 512, 256, 128):
        if d_ff % c == 0:
            return c
    return d_ff


def _dense_ffn(x, wg, wu, wd, ln_g, ln_b, *, tm, alpha):
    n, d = x.shape
    row = lambda i: (i, 0)
    return pl.pallas_call(
        functools.partial(_ffn_kernel, alpha=alpha, ff_chunk=_ff_chunk(wg.shape[1])),
        grid=(n // tm,),
        in_specs=[pl.BlockSpec((tm, d), row), _resident(wg.shape), _resident(wu.shape),
                  _resident(wd.shape), _resident(ln_g.shape), _resident(ln_b.shape)],
        out_specs=pl.BlockSpec((tm, d), row),
        out_shape=jax.ShapeDtypeStruct((n, d), F32),
        compiler_params=_cparams(("arbitrary",)), name="dense_ffn",
    )(x, wg, wu, wd, ln_g, ln_b)


def _moe_kernel(x_ref, router_ref, wg_ref, wu_ref, wd_ref, g_ref, b_ref, out_ref,
                acc_ref, comb_ref, xb_ref, *, alpha, n_experts):
    e = pl.program_id(1)
    c = pl.program_id(2)

    @pl.when((e == 0) & (c == 0))
    def _():
        x = x_ref[...]
        xb_ref[...] = x.astype(BF16)
        acc_ref[...] = jnp.zeros_like(acc_ref)
        logits = jnp.dot(x, router_ref[...], preferred_element_type=F32,
                         precision=lax.Precision.HIGHEST)
        lane = lax.broadcasted_iota(jnp.int32, logits.shape, 1)
        logits = jnp.where(lane < n_experts, logits, -jnp.inf)
        m1 = jnp.max(logits, axis=1, keepdims=True)
        i1 = jnp.min(jnp.where(logits == m1, lane, LANES), axis=1, keepdims=True)
        rest = jnp.where(lane == i1, -jnp.inf, logits)
        m2 = jnp.max(rest, axis=1, keepdims=True)
        i2 = jnp.min(jnp.where(rest == m2, lane, LANES), axis=1, keepdims=True)
        t = jnp.exp(m2 - m1)
        w1 = 1.0 / (1.0 + t)
        comb_ref[...] = jnp.where(lane == i1, w1, 0.0) + jnp.where(lane == i2, t * w1, 0.0)

    comb = comb_ref[...]
    lane = lax.broadcasted_iota(jnp.int32, comb.shape, 1)
    w_e = jnp.sum(jnp.where(lane == e, comb, 0.0), axis=1, keepdims=True)
    xb = xb_ref[...]
    h = jax.nn.silu(_dot(xb, wg_ref[...])) * _dot(xb, wu_ref[...])
    acc_ref[...] += w_e * _dot(h.astype(BF16), wd_ref[...])

    @pl.when((e == pl.num_programs(1) - 1) & (c == pl.num_programs(2) - 1))
    def _():
        out_ref[...] = _layer_norm(alpha * x_ref[...] + acc_ref[...], g_ref[...], b_ref[...])


def _moe_ffn(x, router_pad, wg, wu, wd, ln_g, ln_b, *, tm, alpha, n_experts):
    n, d = x.shape
    d_ff = wg.shape[2]
    fc = _ff_chunk(d_ff)
    row = lambda i, e, c: (i, 0)
    return pl.pallas_call(
        functools.partial(_moe_kernel, alpha=alpha, n_experts=n_experts),
        grid=(n // tm, n_experts, d_ff // fc),
        in_specs=[pl.BlockSpec((tm, d), row),
                  pl.BlockSpec(router_pad.shape, lambda i, e, c: (0, 0)),
                  pl.BlockSpec((None, d, fc), lambda i, e, c: (e, 0, c)),
                  pl.BlockSpec((None, d, fc), lambda i, e, c: (e, 0, c)),
                  pl.BlockSpec((None, fc, d), lambda i, e, c: (e, c, 0)),
                  pl.BlockSpec(ln_g.shape, lambda i, e, c: (0, 0)),
                  pl.BlockSpec(ln_b.shape, lambda i, e, c: (0, 0))],
        out_specs=pl.BlockSpec((tm, d), row),
        out_shape=jax.ShapeDtypeStruct((n, d), F32),
        scratch_shapes=[pltpu.VMEM((tm, d), F32), pltpu.VMEM((tm, LANES), F32),
                        pltpu.VMEM((tm, d), BF16)],
        compiler_params=_cparams(("arbitrary", "arbitrary", "arbitrary")), name="moe_ffn",
    )(x, router_pad, wg, wu, wd, ln_g, ln_b)


def _token_tile(n, cap):
    t = cap
    while n % t:
        t //= 2
    return t


def _alibi_sigma():
    h = jnp.arange(1, N_HEADS + 1, dtype=F32)
    return jnp.exp2(-8.0 * h / N_HEADS) * LOG2E


def kernel(x_prompt, x_sample, cache_k, cache_v, state_conv, page_table, w_in, lam_params, subln_g, gmlp_ln_g, gmlp_ln_b, gmlp_ws, gmlp_bs, conv_w, p_attn, p_gmlp, p_conv, w_o, ln1_g, ln1_b, ln2_g, ln2_b, ffn_gate, ffn_up, ffn_down, router, moe_gate, moe_up, moe_down):
    depth = w_in.shape[0]
    nb_p, seq_p, d_model = x_prompt.shape
    nb_s, seq_s, _ = x_sample.shape
    n_pool, page_size = cache_k.shape[1], cache_k.shape[2]
    n_pages = page_table.shape[1]
    past_len = n_pages * page_size
    n_experts = router.shape[2]
    alpha = (2 * depth) ** 0.25
    d_q = N_HEADS * QK_DIM
    d_kv = N_KV_HEADS * QK_DIM
    d_qkv = d_q + 2 * d_kv
    cd = p_conv.shape[1]
    q_scale = LOG2E * HEAD_DIM ** -0.5

    n_p = nb_p * seq_p
    n_s = nb_s * seq_s
    tm_p = _token_tile(seq_p, 512)
    tm_s = n_s
    tq = ATTN_BLOCK

    sigma = _alibi_sigma()
    rel = (jnp.arange(tq, dtype=F32)[None, :] - jnp.arange(tq, dtype=F32)[:, None])
    sig_gr = sigma.reshape(N_KV_HEADS, REP, 1, 1)
    bias_off = -sig_gr * rel
    bias_diag = jnp.where(rel >= 0, bias_off, NEG_BIG)

    n_cols = 2 * N_KV_HEADS * REP * seq_s
    sig_cols = jnp.broadcast_to(sigma.reshape(1, N_KV_HEADS, REP, 1), (2, N_KV_HEADS, REP, seq_s)).reshape(n_cols)
    t_cols = jnp.broadcast_to(jnp.arange(seq_s, dtype=F32), (2, N_KV_HEADS, REP, seq_s)).reshape(n_cols)
    pad_cols = LANES - n_cols
    sig_cols = jnp.pad(sig_cols, (0, pad_cols))
    t_cols = jnp.pad(t_cols, (0, pad_cols))
    key_row = jnp.arange(page_size, dtype=F32)[:, None]
    base_bias = -sig_cols[None, :] * (past_len + t_cols[None, :] - key_row)
    pagevec = (sig_cols * page_size)[None, :]
    new_row = jnp.arange(NEW_ROWS, dtype=F32)[:, None]
    bias_new = jnp.where((new_row <= t_cols[None, :]) & (new_row < seq_s),
                         -sig_cols[None, :] * (t_cols[None, :] - new_row), NEG_BIG)

    cache_k4 = cache_k.reshape(depth, n_pool, page_size, d_kv)
    cache_v4 = cache_v.reshape(depth, n_pool, page_size, d_kv)

    ws_tril = jnp.tril(gmlp_ws)
    eye_s = jnp.eye(CHUNK // seq_s, dtype=F32)

    row2 = lambda a: a.reshape(1, -1)
    router_pad = jnp.pad(router, ((0, 0), (0, 0), (0, LANES - n_experts)))

    yp = x_prompt.reshape(n_p, d_model)
    ys = x_sample.reshape(n_s, d_model)
    outs = {k: [] for k in ("kp", "vp", "cp", "ks", "vs", "cs", "gs")}
    for l in range(depth):
        lam_init = 0.8 - 0.6 * math.exp(-0.3 * l)
        lp = lam_params[l].astype(F32)
        lam = (jnp.exp(jnp.sum(lp[0] * lp[1])) - jnp.exp(jnp.sum(lp[2] * lp[3])) + lam_init).reshape(1)
        out_scale = 1.0 - lam_init
        w_l = w_in[l].astype(BF16)
        w_qkv, w_br = w_l[:, :d_qkv], w_l[:, d_qkv:]
        pa, pg, pc, wo = (p_attn[l].astype(BF16), p_gmlp[l].astype(BF16),
                          p_conv[l].astype(BF16), w_o[l].astype(BF16))
        ws_p = ws_tril[l].astype(BF16)
        bs_p = jnp.broadcast_to(gmlp_bs[l][:, :, None], (GMLP_GROUPS, CHUNK, LANES))
        ws_s = jnp.einsum("ab,gij->gaibj", eye_s, ws_tril[l][:, :seq_s, :seq_s]).reshape(
            GMLP_GROUPS, CHUNK, CHUNK).astype(BF16)
        bs_s = jnp.broadcast_to(jnp.tile(gmlp_bs[l][:, :seq_s], (1, CHUNK // seq_s))[:, :, None],
                                (GMLP_GROUPS, CHUNK, LANES))

        def mixer(x, tm):
            if l % 2 == 0:
                i = l // 2
                return _dense_ffn(x, ffn_gate[i].astype(BF16), ffn_up[i].astype(BF16),
                                  ffn_down[i].astype(BF16), row2(ln2_g[l]), row2(ln2_b[l]),
                                  tm=tm, alpha=alpha)
            i = l // 2
            return _moe_ffn(x, router_pad[i], moe_gate[i].astype(BF16), moe_up[i].astype(BF16),
                            moe_down[i].astype(BF16), row2(ln2_g[l]), row2(ln2_b[l]),
                            tm=tm, alpha=alpha, n_experts=n_experts)

        q0, q1, kf, vf, kb, vt = _qkv_proj(yp, w_qkv, prompt=True, tm=tm_p, seq_len=seq_p, tk=tq,
                                            q_scale=q_scale)
        o = _prompt_attention(lam, sigma, q0, q1, kb, vt, bias_off, bias_diag,
                              subln_g[l].reshape(V_DIM, 1), seq_len=seq_p, out_scale=out_scale)
        part, ga, ztail = _branch_proj(yp, w_br, row2(gmlp_ln_g[l]), row2(gmlp_ln_b[l]), ws_p, bs_p,
                                       conv_w[l], pg, pc, None, prompt=True, tm=tm_p,
                                       seq_len=seq_p, period=1)
        x1 = _mix_proj(yp, o, ga, part, pa, wo, row2(ln1_g[l]), row2(ln1_b[l]), tm=tm_p, alpha=alpha)
        yp = mixer(x1, tm_p)
        outs["kp"].append(kf.reshape(nb_p, seq_p, N_KV_HEADS, QK_DIM))
        outs["vp"].append(vf.reshape(nb_p, seq_p, N_KV_HEADS, V_DIM))
        tiles_per_seq = seq_p // tm_p
        outs["cp"].append(ztail.reshape(nb_p, tiles_per_seq, SUBLANES, cd)[:, -1, SUBLANES - (CONV_WIDTH - 1):])

        qs, kfs, vfs = _qkv_proj(ys, w_qkv, prompt=False, tm=tm_s, seq_len=seq_s, tk=tq, q_scale=q_scale)
        q6 = qs.reshape(nb_s, seq_s, N_KV_HEADS, REP, 2, HEAD_DIM)
        eye_g = jnp.eye(N_KV_HEADS, dtype=F32)
        eye_m = jnp.eye(2, dtype=F32)
        qbd = jnp.einsum("btgrmd,gh,mn->bgmdnhrt", q6, eye_g, eye_m).reshape(nb_s, d_kv, n_cols)
        qbd = jnp.pad(qbd, ((0, 0), (0, 0), (0, pad_cols))).astype(BF16)
        pad_new = ((0, 0), (0, NEW_ROWS - seq_s), (0, 0))
        knew = jnp.pad(kfs.reshape(nb_s, seq_s, d_kv), pad_new)
        vnew = jnp.pad(vfs.reshape(nb_s, seq_s, d_kv), pad_new)
        o_s = _sample_attention(page_table, lam, qbd, cache_k4, cache_v4, l, base_bias, pagevec,
                                knew, vnew, bias_new, subln_g[l].reshape(1, V_DIM), seq_s=seq_s,
                                out_scale=out_scale)
        o_s = o_s.reshape(nb_s, N_KV_HEADS, REP, seq_s, V_DIM).transpose(0, 3, 1, 2, 4).reshape(n_s, N_HEADS * V_DIM)
        prev = state_conv[l]
        zeros = jnp.zeros((nb_s, seq_s - 1, cd), F32)
        fix1 = jnp.concatenate([prev[:, 1:2], zeros], axis=1).reshape(n_s, cd)
        fix2 = jnp.concatenate([prev, zeros[:, 1:]], axis=1).reshape(n_s, cd)
        part_s, ga_s, z_s, zv_s = _branch_proj(ys, w_br, row2(gmlp_ln_g[l]), row2(gmlp_ln_b[l]), ws_s,
                                               bs_s, conv_w[l], pg, pc, (fix1, fix2), prompt=False,
                                               tm=tm_s, seq_len=seq_s, period=seq_s)
        x1s = _mix_proj(ys, o_s, ga_s, part_s, pa, wo, row2(ln1_g[l]), row2(ln1_b[l]), tm=tm_s, alpha=alpha)
        ys = mixer(x1s, tm_s)
        outs["ks"].append(kfs.reshape(nb_s, seq_s, N_KV_HEADS, QK_DIM))
        outs["vs"].append(vfs.reshape(nb_s, seq_s, N_KV_HEADS, V_DIM))
        outs["cs"].append(z_s.reshape(nb_s, seq_s, cd)[:, seq_s - (CONV_WIDTH - 1):])
        outs["gs"].append(zv_s.reshape(nb_s, seq_s, -1))

    return (yp.reshape(nb_p, seq_p, d_model), ys.reshape(nb_s, seq_s, d_model),
            jnp.stack(outs["kp"]), jnp.stack(outs["vp"]), jnp.stack(outs["cp"]),
            jnp.stack(outs["ks"]), jnp.stack(outs["vs"]), jnp.stack(outs["cs"]), jnp.stack(outs["gs"]))
```

```python
import functools
import math

import jax
import jax.numpy as jnp
from jax import lax
from jax.experimental import pallas as pl
from jax.experimental.pallas import tpu as pltpu

F32 = jnp.float32
BF16 = jnp.bfloat16

N_HEADS = 8
N_KV_HEADS = 4
REP = N_HEADS // N_KV_HEADS
HEAD_DIM = 64
QK_DIM = 2 * HEAD_DIM
V_DIM = 2 * HEAD_DIM
GMLP_GROUPS = 4
CHUNK = 128
CONV_WIDTH = 3
N_BRANCH = 3
LN_EPS = 1e-5
LOG2E = 1.4426950408889634
NEG_BIG = -1e30

LANES = 128
SUBLANES = 8
VMEM_LIMIT = 56 * 1024 * 1024

ATTN_BLOCK = 256
PAGES_PER_STEP = 8
MOE_TILE = 1024
ONES_ROWS = 16
NEW_ROWS = 16


def _cparams(sem):
    return pltpu.CompilerParams(dimension_semantics=sem, vmem_limit_bytes=VMEM_LIMIT)


def _resident(shape):
    nd = len(shape)
    return pl.BlockSpec(shape, lambda *_: (0,) * nd, pipeline_mode=pl.Buffered(1))


def _layer_norm(v, g, b):
    mu = jnp.mean(v, axis=-1, keepdims=True)
    d = v - mu
    var = jnp.mean(d * d, axis=-1, keepdims=True)
    return d * lax.rsqrt(var + LN_EPS) * g + b


def _dot(a, b):
    return jnp.dot(a, b, preferred_element_type=F32)


def _qkv_kernel(x_ref, w_ref, *out_refs, prompt, tk, q_scale):
    d_q = N_HEADS * QK_DIM
    d_k = N_KV_HEADS * QK_DIM
    xb = x_ref[...].astype(BF16)
    q = _dot(xb, w_ref[:, 0:d_q]) * q_scale
    k = _dot(xb, w_ref[:, d_q:d_q + d_k])
    v = _dot(xb, w_ref[:, d_q + d_k:])
    if prompt:
        q0_ref, q1_ref, kf_ref, vf_ref, kb_ref, vt_ref = out_refs
        first_half = (lax.broadcasted_iota(jnp.int32, q.shape, 1) % QK_DIM) < HEAD_DIM
        q0_ref[...] = jnp.where(first_half, q, 0.0).astype(BF16)
        q1_ref[...] = jnp.where(first_half, 0.0, q).astype(BF16)
        kf_ref[...] = k
        vf_ref[...] = v
        kb_ref[...] = k.astype(BF16)
        for g in range(N_KV_HEADS):
            for j in range(v.shape[0] // tk):
                blk = v[j * tk:(j + 1) * tk, g * V_DIM:(g + 1) * V_DIM]
                ones = jnp.ones((ONES_ROWS, tk), F32)
                vt_ref[g, j] = jnp.concatenate([blk.T, ones], axis=0).astype(BF16)
    else:
        q_ref, kf_ref, vf_ref = out_refs
        q_ref[...] = q
        kf_ref[...] = k
        vf_ref[...] = v


def _qkv_proj(x, w_qkv, *, prompt, tm, seq_len, tk, q_scale):
    n, d = x.shape
    d_q = N_HEADS * QK_DIM
    d_kv = N_KV_HEADS * QK_DIM
    nt = n // tm
    row = lambda i: (i, 0)
    in_specs = [pl.BlockSpec((tm, d), row), _resident(w_qkv.shape)]
    if prompt:
        tps = seq_len // tm
        nb = n // seq_len
        out_shape = [
            jax.ShapeDtypeStruct((n, d_q), BF16), jax.ShapeDtypeStruct((n, d_q), BF16),
            jax.ShapeDtypeStruct((n, d_kv), F32), jax.ShapeDtypeStruct((n, d_kv), F32),
            jax.ShapeDtypeStruct((n, d_kv), BF16),
            jax.ShapeDtypeStruct((nb, N_KV_HEADS, seq_len // tk, V_DIM + ONES_ROWS, tk), BF16),
        ]
        out_specs = [
            pl.BlockSpec((tm, d_q), row), pl.BlockSpec((tm, d_q), row),
            pl.BlockSpec((tm, d_kv), row), pl.BlockSpec((tm, d_kv), row),
            pl.BlockSpec((tm, d_kv), row),
            pl.BlockSpec((None, N_KV_HEADS, tm // tk, V_DIM + ONES_ROWS, tk),
                         lambda i: (i // tps, 0, i % tps, 0, 0)),
        ]
    else:
        out_shape = [jax.ShapeDtypeStruct((n, d_q), F32),
                     jax.ShapeDtypeStruct((n, d_kv), F32), jax.ShapeDtypeStruct((n, d_kv), F32)]
        out_specs = [pl.BlockSpec((tm, d_q), row), pl.BlockSpec((tm, d_kv), row),
                     pl.BlockSpec((tm, d_kv), row)]
    return pl.pallas_call(
        functools.partial(_qkv_kernel, prompt=prompt, tk=tk, q_scale=q_scale),
        grid=(nt,), in_specs=in_specs, out_specs=out_specs, out_shape=out_shape,
        compiler_params=_cparams(("arbitrary",)), name="qkv_proj",
    )(x, w_qkv)


def _prompt_attn_kernel(lam_ref, sig_ref, q0_ref, q1_ref, k_ref, vt_ref, boff_ref, bdiag_ref,
                        gain_ref, o_ref, acc_ref, s_ref, *, tq, tk, out_scale):
    g = pl.program_id(1)
    qi = pl.program_id(2)
    n_state = REP * 2

    def scores(j, h):
        r, m = divmod(h, 2)
        kj = k_ref[pl.ds(pl.multiple_of(j * tk, tk), tk), :]
        qh = (q0_ref if m == 0 else q1_ref)[:, r * QK_DIM:(r + 1) * QK_DIM]
        return lax.dot_general(kj, qh, (((1,), (1,)), ((), ())), preferred_element_type=F32)

    def block(j, bias_ref, carry, prefetch):
        m_all = carry
        vtj = vt_ref[j]
        dist_blocks = (qi - j).astype(F32) * float(tq)
        m_out = []
        for h in range(n_state):
            r = h // 2
            c_j = -sig_ref[g * REP + r] * dist_blocks
            s_t = s_ref[h] + bias_ref[r]
            if prefetch:
                s_ref[h] = scores(j + 1, h)
            m_new = jnp.maximum(m_all[h], jnp.max(s_t, axis=0, keepdims=True) + c_j)
            alpha = jnp.exp2(m_all[h] - m_new)
            p_t = jnp.exp2(s_t - (m_new - c_j))
            m_out.append(m_new)
            acc_ref[h] = acc_ref[h] * alpha + _dot(vtj, p_t.astype(BF16))
        return tuple(m_out)

    acc_ref[...] = jnp.zeros_like(acc_ref)
    for h in range(n_state):
        s_ref[h] = scores(0, h)
    init = tuple(jnp.full((1, tq), NEG_BIG, F32) for _ in range(n_state))
    def pair(jj, c):
        c = block(2 * jj, boff_ref, c, True)
        return block(2 * jj + 1, boff_ref, c, True)

    carry = lax.fori_loop(0, qi // 2, pair, init)
    carry = lax.cond(qi % 2 == 1, lambda c: block(qi - 1, boff_ref, c, True), lambda c: c, carry)
    block(qi, bdiag_ref, carry, False)

    lam = lam_ref[0]
    gain = gain_ref[...] * out_scale
    for r in range(REP):
        a0, a1 = acc_ref[2 * r], acc_ref[2 * r + 1]
        o_t = a0[:V_DIM] / a0[V_DIM:V_DIM + 1] - lam * (a1[:V_DIM] / a1[V_DIM:V_DIM + 1])
        ms = jnp.mean(o_t * o_t, axis=0, keepdims=True)
        o_t = o_t * lax.rsqrt(ms + LN_EPS) * gain
        o_ref[:, r * V_DIM:(r + 1) * V_DIM] = o_t.T.astype(o_ref.dtype)


def _prompt_attention(lam, sigma, q0, q1, kb, vt, bias_off, bias_diag, gain, *, seq_len, out_scale):
    n = q0.shape[0]
    nb = n // seq_len
    tq = tk = ATTN_BLOCK
    nq = seq_len // tq
    smem = pl.BlockSpec(memory_space=pltpu.SMEM)
    qspec = pl.BlockSpec((tq, REP * QK_DIM), lambda b, g, i: (b * nq + i, g))
    return pl.pallas_call(
        functools.partial(_prompt_attn_kernel, tq=tq, tk=tk, out_scale=out_scale),
        grid=(nb, N_KV_HEADS, nq),
        in_specs=[
            smem, smem, qspec, qspec,
            pl.BlockSpec((seq_len, QK_DIM), lambda b, g, i: (b, g)),
            pl.BlockSpec((None, None, seq_len // tk, V_DIM + ONES_ROWS, tk),
                         lambda b, g, i: (b, g, 0, 0, 0)),
            pl.BlockSpec((None, REP, tk, tq), lambda b, g, i: (g, 0, 0, 0)),
            pl.BlockSpec((None, REP, tk, tq), lambda b, g, i: (g, 0, 0, 0)),
            pl.BlockSpec((V_DIM, 1), lambda b, g, i: (0, 0)),
        ],
        out_specs=pl.BlockSpec((tq, REP * V_DIM), lambda b, g, i: (b * nq + i, g)),
        out_shape=jax.ShapeDtypeStruct((n, N_HEADS * V_DIM), BF16),
        scratch_shapes=[pltpu.VMEM((REP * 2, V_DIM + ONES_ROWS, tq), F32),
                        pltpu.VMEM((REP * 2, tk, tq), F32)],
        compiler_params=_cparams(("arbitrary", "arbitrary", "arbitrary")), name="prompt_attn",
    )(lam, sigma, q0, q1, kb, vt, bias_off, bias_diag, gain)


def _sample_attn_kernel(pt_ref, lam_ref, qbd_ref, *refs, n_pages_step, out_scale):
    del pt_ref
    P = n_pages_step
    k_refs = refs[:P]
    v_refs = refs[P:2 * P]
    (base_ref, pagevec_ref, knew_ref, vnew_ref, bnew_ref, gain_ref,
     o_ref, m_ref, l_ref, acc_ref) = refs[2 * P:]
    c = pl.program_id(1)
    n_c = pl.num_programs(1)
    rows_g = o_ref.shape[1]
    half = N_KV_HEADS * rows_g

    @pl.when(c == 0)
    def _():
        m_ref[...] = jnp.full_like(m_ref, NEG_BIG)
        l_ref[...] = jnp.zeros_like(l_ref)
        acc_ref[...] = jnp.zeros_like(acc_ref)

    qbd = qbd_ref[...]

    def col_bcast(row):
        return jnp.transpose(jnp.broadcast_to(row, (LANES, LANES)))

    def update(s_tiles, v_tiles):
        m_old = m_ref[...]
        mx = s_tiles[0].max(axis=0, keepdims=True)
        for s_t in s_tiles[1:]:
            mx = jnp.maximum(mx, s_t.max(axis=0, keepdims=True))
        m_new = jnp.maximum(m_old, mx)
        alpha = jnp.exp2(m_old - m_new)
        lsum = jnp.zeros_like(m_old)
        pv = None
        for s_t, v_t in zip(s_tiles, v_tiles):
            p_t = jnp.exp2(s_t - m_new)
            lsum = lsum + jnp.sum(p_t, axis=0, keepdims=True)
            part = lax.dot_general(p_t.astype(BF16), v_t, (((0,), (0,)), ((), ())),
                                   preferred_element_type=F32)
            pv = part if pv is None else pv + part
        m_ref[...] = m_new
        l_ref[...] = alpha * l_ref[...] + lsum
        acc_ref[...] = acc_ref[...] * col_bcast(alpha) + pv

    base = base_ref[...]
    page0 = (c * P).astype(F32)
    s_tiles, v_tiles = [], []
    for p in range(P):
        s_t = _dot(k_refs[p][...].astype(BF16), qbd) + base + pagevec_ref[...] * (page0 + float(p))
        s_tiles.append(s_t)
        v_tiles.append(v_refs[p][...].astype(BF16))
    update(s_tiles, v_tiles)

    @pl.when(c == n_c - 1)
    def _():
        s_new = _dot(knew_ref[...].astype(BF16), qbd) + bnew_ref[...]
        update([s_new], [vnew_ref[...].astype(BF16)])
        l_col = col_bcast(l_ref[...])
        acc = acc_ref[...]
        o = acc[:half] / l_col[:half] - lam_ref[0] * (acc[half:2 * half] / l_col[half:2 * half])
        ms = jnp.mean(o * o, axis=-1, keepdims=True)
        o = o * lax.rsqrt(ms + LN_EPS) * (gain_ref[...] * out_scale)
        for g in range(N_KV_HEADS):
            o_ref[g] = o[g * rows_g:(g + 1) * rows_g].astype(o_ref.dtype)


def _sample_attention(page_table, lam, qbd, cache_k, cache_v, layer, base, pagevec, knew, vnew,
                      bias_new, gain, *, seq_s, out_scale):
    nb, n_pages = page_table.shape
    P = PAGES_PER_STEP
    page_shape = cache_k.shape[2:]
    rows_g = REP * seq_s
    smem = pl.BlockSpec(memory_space=pltpu.SMEM)

    def page_spec(p):
        return pl.BlockSpec((None, None) + page_shape,
                            lambda b, c, pt: (layer, pt[b, c * P + p], 0, 0))

    const2 = lambda b, c, pt: (0, 0)
    per_seq = lambda b, c, pt: (b, 0, 0)
    grid_spec = pltpu.PrefetchScalarGridSpec(
        num_scalar_prefetch=1,
        grid=(nb, n_pages // P),
        in_specs=[smem, pl.BlockSpec((None,) + qbd.shape[1:], per_seq)]
        + [page_spec(p) for p in range(P)] + [page_spec(p) for p in range(P)]
        + [pl.BlockSpec(base.shape, const2), pl.BlockSpec(pagevec.shape, const2),
           pl.BlockSpec((None,) + knew.shape[1:], per_seq),
           pl.BlockSpec((None,) + vnew.shape[1:], per_seq),
           pl.BlockSpec(bias_new.shape, const2), pl.BlockSpec(gain.shape, const2)],
        out_specs=pl.BlockSpec((None, N_KV_HEADS, rows_g, V_DIM), lambda b, c, pt: (b, 0, 0, 0)),
        scratch_shapes=[pltpu.VMEM((1, LANES), F32), pltpu.VMEM((1, LANES), F32),
                        pltpu.VMEM((LANES, V_DIM), F32)],
    )
    return pl.pallas_call(
        functools.partial(_sample_attn_kernel, n_pages_step=P, out_scale=out_scale),
        grid_spec=grid_spec,
        out_shape=jax.ShapeDtypeStruct((nb, N_KV_HEADS, rows_g, V_DIM), BF16),
        compiler_params=_cparams(("arbitrary", "arbitrary")), name="sample_attn",
    )(page_table, lam, qbd, *([cache_k] * P), *([cache_v] * P), base, pagevec, knew, vnew,
      bias_new, gain)


def _branch_kernel(x_ref, w_ref, lng_ref, lnb_ref, ws_ref, bs_ref, cw_ref, pg_ref, pc_ref, *refs,
                   prompt, tiles_per_seq, period):
    d_model = x_ref.shape[1]
    gd = pg_ref.shape[0]
    cd = pc_ref.shape[0]
    tm = x_ref.shape[0]
    if prompt:
        part_ref, ga_ref, ztail_ref, carry_ref = refs
    else:
        fix1_ref, fix2_ref, part_ref, ga_ref, z_ref, zv_ref = refs
    xb = x_ref[...].astype(BF16)

    def proj(lo, width):
        return _dot(xb, w_ref[:, lo:lo + width])

    z_u = jax.nn.gelu(proj(0, gd))
    z_v = _layer_norm(jax.nn.gelu(proj(gd, gd)), lng_ref[...], lnb_ref[...])
    if not prompt:
        zv_ref[...] = z_v
    gw = gd // GMLP_GROUPS
    z_vb = z_v.astype(BF16)
    rows = []
    for n in range(tm // CHUNK):
        cols = []
        for g in range(GMLP_GROUPS):
            blk = z_vb[n * CHUNK:(n + 1) * CHUNK, g * gw:(g + 1) * gw]
            cols.append(_dot(ws_ref[g], blk) + bs_ref[g])
        rows.append(jnp.concatenate(cols, axis=1))
    s = jnp.concatenate(rows, axis=0)
    y_g = _dot((z_u * s).astype(BF16), pg_ref[...])

    off = 2 * gd
    c_b = proj(off, cd)
    z = proj(off + cd, cd) * proj(off + 2 * cd, cd)
    zr1 = pltpu.roll(z, 1, 0)
    zr2 = pltpu.roll(z, 2, 0)
    if prompt:
        @pl.when(pl.program_id(0) % tiles_per_seq == 0)
        def _():
            carry_ref[...] = jnp.zeros_like(carry_ref)

        prev = carry_ref[...]
        r8 = lax.broadcasted_iota(jnp.int32, (SUBLANES, cd), 0)
        top1 = jnp.where(r8 < 1, pltpu.roll(prev, 1, 0), zr1[:SUBLANES])
        top2 = jnp.where(r8 < 2, pltpu.roll(prev, 2, 0), zr2[:SUBLANES])
        zm1 = jnp.concatenate([top1, zr1[SUBLANES:]], axis=0)
        zm2 = jnp.concatenate([top2, zr2[SUBLANES:]], axis=0)
        tail = z[tm - SUBLANES:]
        carry_ref[...] = tail
        ztail_ref[...] = tail
    else:
        pos = lax.broadcasted_iota(jnp.int32, z.shape, 0) % period
        zm1 = jnp.where(pos >= 1, zr1, fix1_ref[...])
        zm2 = jnp.where(pos >= 2, zr2, fix2_ref[...])
        z_ref[...] = z
    cw = cw_ref[...]
    y = cw[0:1] * zm2 + cw[1:2] * zm1 + cw[2:3] * z
    y_c = _dot((c_b * y).astype(BF16), pc_ref[...])

    off = 2 * gd + 3 * cd
    ga_ref[...] = jax.nn.sigmoid(proj(off, d_model)).astype(ga_ref.dtype)
    part_ref[...] = (jax.nn.sigmoid(proj(off + d_model, d_model)) * y_g
                     + jax.nn.sigmoid(proj(off + 2 * d_model, d_model)) * y_c)


def _branch_proj(x, w_br, ln_g, ln_b, ws_mat, bs_mat, conv_w, p_gmlp, p_conv, fixes, *,
                 prompt, tm, seq_len, period):
    n, d = x.shape
    gd, cd = p_gmlp.shape[0], p_conv.shape[0]
    nt = n // tm
    row = lambda i: (i, 0)
    in_specs = [pl.BlockSpec((tm, d), row), _resident(w_br.shape), _resident(ln_g.shape),
                _resident(ln_b.shape), _resident(ws_mat.shape), _resident(bs_mat.shape),
                _resident(conv_w.shape), _resident(p_gmlp.shape), _resident(p_conv.shape)]
    args = [x, w_br, ln_g, ln_b, ws_mat, bs_mat, conv_w, p_gmlp, p_conv]
    out_shape = [jax.ShapeDtypeStruct((n, d), F32), jax.ShapeDtypeStruct((n, d), BF16)]
    out_specs = [pl.BlockSpec((tm, d), row), pl.BlockSpec((tm, d), row)]
    scratch = []
    if prompt:
        out_shape.append(jax.ShapeDtypeStruct((nt * SUBLANES, cd), F32))
        out_specs.append(pl.BlockSpec((SUBLANES, cd), row))
        scratch.append(pltpu.VMEM((SUBLANES, cd), F32))
    else:
        in_specs += [pl.BlockSpec((tm, cd), row), pl.BlockSpec((tm, cd), row)]
        args += list(fixes)
        out_shape += [jax.ShapeDtypeStruct((n, cd), F32), jax.ShapeDtypeStruct((n, gd), F32)]
        out_specs += [pl.BlockSpec((tm, cd), row), pl.BlockSpec((tm, gd), row)]
    return pl.pallas_call(
        functools.partial(_branch_kernel, prompt=prompt,
                          tiles_per_seq=max(seq_len // tm, 1), period=period),
        grid=(nt,), in_specs=in_specs, out_specs=out_specs, out_shape=out_shape,
        scratch_shapes=scratch,
        compiler_params=_cparams(("arbitrary",)), name="branch_proj",
    )(*args)


def _mix_kernel(x_ref, o_ref, ga_ref, part_ref, pa_ref, wo_ref, g_ref, b_ref, out_ref, *, alpha):
    y_a = _dot(o_ref[...], pa_ref[...])
    mix = ga_ref[...].astype(F32) * y_a + part_ref[...]
    h = _dot(mix.astype(BF16), wo_ref[...])
    out_ref[...] = _layer_norm(alpha * x_ref[...] + h, g_ref[...], b_ref[...])


def _mix_proj(x, o, ga, part, p_attn, w_o, ln_g, ln_b, *, tm, alpha):
    n, d = x.shape
    row = lambda i: (i, 0)
    return pl.pallas_call(
        functools.partial(_mix_kernel, alpha=alpha),
        grid=(n // tm,),
        in_specs=[pl.BlockSpec((tm, d), row), pl.BlockSpec((tm, o.shape[1]), row),
                  pl.BlockSpec((tm, d), row), pl.BlockSpec((tm, d), row),
                  _resident(p_attn.shape), _resident(w_o.shape),
                  _resident(ln_g.shape), _resident(ln_b.shape)],
        out_specs=pl.BlockSpec((tm, d), row),
        out_shape=jax.ShapeDtypeStruct((n, d), F32),
        compiler_params=_cparams(("arbitrary",)), name="mix_proj",
    )(x, o, ga, part, p_attn, w_o, ln_g, ln_b)


def _ffn_kernel(x_ref, wg_ref, wu_ref, wd_ref, g_ref, b_ref, out_ref, *, alpha, ff_chunk):
    x = x_ref[...]
    xb = x.astype(BF16)
    f = None
    for c0 in range(0, wg_ref.shape[1], ff_chunk):
        h = jax.nn.silu(_dot(xb, wg_ref[:, c0:c0 + ff_chunk])) * _dot(xb, wu_ref[:, c0:c0 + ff_chunk])
        part = _dot(h.astype(BF16), wd_ref[c0:c0 + ff_chunk, :])
        f = part if f is None else f + part
    out_ref[...] = _layer_norm(alpha * x + f, g_ref[...], b_ref[...])


def _ff_chunk(d_ff):
    for c in (1408, 1024, 512, 256, 128):
        if d_ff % c == 0:
            return c
    return d_ff


def _dense_ffn(x, wg, wu, wd, ln_g, ln_b, *, tm, alpha):
    n, d = x.shape
    row = lambda i: (i, 0)
    return pl.pallas_call(
        functools.partial(_ffn_kernel, alpha=alpha, ff_chunk=_ff_chunk(wg.shape[1])),
        grid=(n // tm,),
        in_specs=[pl.BlockSpec((tm, d), row), _resident(wg.shape), _resident(wu.shape),
                  _resident(wd.shape), _resident(ln_g.shape), _resident(ln_b.shape)],
        out_specs=pl.BlockSpec((tm, d), row),
        out_shape=jax.ShapeDtypeStruct((n, d), F32),
        compiler_params=_cparams(("arbitrary",)), name="dense_ffn",
    )(x, wg, wu, wd, ln_g, ln_b)


def _router_kernel(x_ref, r_ref, tri_ref, comb_ref, rank_ref, cnt_ref, *, n_experts):
    logits = jnp.dot(x_ref[...], r_ref[...], preferred_element_type=F32,
                     precision=lax.Precision.HIGHEST)
    lt = logits.T[:n_experts]
    row = lax.broadcasted_iota(jnp.int32, lt.shape, 0)
    m1 = jnp.max(lt, axis=0, keepdims=True)
    i1 = jnp.min(jnp.where(lt == m1, row, n_experts), axis=0, keepdims=True)
    rest = jnp.where(row == i1, -jnp.inf, lt)
    m2 = jnp.max(rest, axis=0, keepdims=True)
    i2 = jnp.min(jnp.where(rest == m2, row, n_experts), axis=0, keepdims=True)
    t = jnp.exp(m2 - m1)
    w1 = 1.0 / (1.0 + t)
    comb = jnp.where(row == i1, w1, 0.0) + jnp.where(row == i2, t * w1, 0.0)
    routed = jnp.where(comb > 0.0, 1.0, 0.0)
    lhs = jnp.concatenate([routed, jnp.zeros_like(routed)], axis=0).astype(BF16)
    comb_ref[...] = comb
    rank_ref[...] = _dot(lhs, tri_ref[...])[:n_experts]
    cnt_ref[...] = jnp.broadcast_to(jnp.sum(routed, axis=1, keepdims=True), cnt_ref.shape)


def _moe_kernel(cnt_ref, x_ref, comb_ref, rank_ref, wg_ref, wu_ref, wd_ref, g_ref, b_ref, out_ref,
                xb_ref, xs_ref, y_ref, *, alpha, rows):
    i = pl.program_id(0)
    e = pl.program_id(1)
    c = pl.program_id(2)
    last_c = pl.num_programs(2) - 1
    t = x_ref.shape[0]
    n_blk = (cnt_ref[i, e] + rows - 1) // rows
    comb_row = comb_ref[pl.ds(e, 1), :]
    rank_row = rank_ref[pl.ds(e, 1), :]

    @pl.when((e == 0) & (c == 0))
    def _():
        xb_ref[...] = x_ref[...].astype(BF16)
        out_ref[...] = jnp.zeros_like(out_ref)

    def block_rows(b):
        return pl.ds(pl.multiple_of(b * rows, rows), rows)

    def selection(b, weighted):
        slot = (lax.broadcasted_iota(jnp.int32, (rows, t), 0) + b * rows).astype(F32)
        hit = (rank_row == slot) & (comb_row > 0.0)
        return jnp.where(hit, comb_row if weighted else 1.0, 0.0).astype(BF16)

    def expert(b):
        xs = xs_ref[block_rows(b), :]
        h = jax.nn.silu(_dot(xs, wg_ref[...])) * _dot(xs, wu_ref[...])
        return _dot(h.astype(BF16), wd_ref[...])

    def for_blocks(body):
        def step(b, carry):
            body(b)
            return carry
        lax.fori_loop(0, n_blk, step, 0)

    @pl.when(c == 0)
    def _():
        def gather_and_run(b):
            xs_ref[block_rows(b), :] = _dot(selection(b, False), xb_ref[...]).astype(BF16)
            y_ref[block_rows(b), :] = expert(b)
        for_blocks(gather_and_run)

    @pl.when(c > 0)
    def _():
        def run(b):
            y_ref[block_rows(b), :] += expert(b)
        for_blocks(run)

    @pl.when(c == last_c)
    def _():
        def scatter(b):
            out_ref[...] += lax.dot_general(selection(b, True), y_ref[block_rows(b), :].astype(BF16),
                                            (((0,), (0,)), ((), ())), preferred_element_type=F32)
        for_blocks(scatter)

    @pl.when((e == pl.num_programs(1) - 1) & (c == last_c))
    def _():
        out_ref[...] = _layer_norm(alpha * x_ref[...] + out_ref[...], g_ref[...], b_ref[...])


def _moe_block_rows(tm, n_experts, top_k=2):
    mean = tm * top_k // n_experts
    return min(tm, -(-(mean + 64) // 64) * 64)


def _moe_ffn(x, router_pad, wg, wu, wd, ln_g, ln_b, *, tm, alpha, n_experts):
    n, d = x.shape
    nt = n // tm
    d_ff = wg.shape[2]
    fc = _ff_chunk(d_ff)
    rows = _moe_block_rows(tm, n_experts)
    cap = -(-tm // rows) * rows
    tri = (jnp.arange(tm)[:, None] < jnp.arange(tm)[None, :]).astype(BF16)
    comb, rank, cnt = pl.pallas_call(
        functools.partial(_router_kernel, n_experts=n_experts),
        grid=(nt,),
        in_specs=[pl.BlockSpec((tm, d), lambda i: (i, 0)), _resident(router_pad.shape),
                  _resident(tri.shape)],
        out_specs=[pl.BlockSpec((n_experts, tm), lambda i: (0, i)),
                   pl.BlockSpec((n_experts, tm), lambda i: (0, i)),
                   pl.BlockSpec((None, n_experts, LANES), lambda i: (i, 0, 0))],
        out_shape=[jax.ShapeDtypeStruct((n_experts, n), F32), jax.ShapeDtypeStruct((n_experts, n), F32),
                   jax.ShapeDtypeStruct((nt, n_experts, LANES), F32)],
        compiler_params=_cparams(("arbitrary",)), name="moe_router",
    )(x, router_pad, tri)
    counts = cnt[:, :, 0].astype(jnp.int32)
    row = lambda i, e, c, cnt: (i, 0)
    col = lambda i, e, c, cnt: (0, i)
    const = lambda i, e, c, cnt: (0, 0)
    grid_spec = pltpu.PrefetchScalarGridSpec(
        num_scalar_prefetch=1,
        grid=(nt, n_experts, d_ff // fc),
        in_specs=[pl.BlockSpec((tm, d), row, pipeline_mode=pl.Buffered(1)),
                  pl.BlockSpec((n_experts, tm), col), pl.BlockSpec((n_experts, tm), col),
                  pl.BlockSpec((None, d, fc), lambda i, e, c, cnt: (e, 0, c)),
                  pl.BlockSpec((None, d, fc), lambda i, e, c, cnt: (e, 0, c)),
                  pl.BlockSpec((None, fc, d), lambda i, e, c, cnt: (e, c, 0)),
                  pl.BlockSpec(ln_g.shape, const), pl.BlockSpec(ln_b.shape, const)],
        out_specs=pl.BlockSpec((tm, d), row),
        scratch_shapes=[pltpu.VMEM((tm, d), BF16), pltpu.VMEM((cap, d), BF16),
                        pltpu.VMEM((cap, d), F32)],
    )
    return pl.pallas_call(
        functools.partial(_moe_kernel, alpha=alpha, rows=rows),
        grid_spec=grid_spec,
        out_shape=jax.ShapeDtypeStruct((n, d), F32),
        compiler_params=_cparams(("arbitrary", "arbitrary", "arbitrary")), name="moe_ffn",
    )(counts, x, comb, rank, wg, wu, wd, ln_g, ln_b)


def _token_tile(n, cap):
    t = cap
    while n % t:
        t //= 2
    return t


def _alibi_sigma():
    h = jnp.arange(1, N_HEADS + 1, dtype=F32)
    return jnp.exp2(-8.0 * h / N_HEADS) * LOG2E


def kernel(x_prompt, x_sample, cache_k, cache_v, state_conv, page_table, w_in, lam_params, subln_g, gmlp_ln_g, gmlp_ln_b, gmlp_ws, gmlp_bs, conv_w, p_attn, p_gmlp, p_conv, w_o, ln1_g, ln1_b, ln2_g, ln2_b, ffn_gate, ffn_up, ffn_down, router, moe_gate, moe_up, moe_down):
    depth = w_in.shape[0]
    nb_p, seq_p, d_model = x_prompt.shape
    nb_s, seq_s, _ = x_sample.shape
    n_pool, page_size = cache_k.shape[1], cache_k.shape[2]
    n_pages = page_table.shape[1]
    past_len = n_pages * page_size
    n_experts = router.shape[2]
    alpha = (2 * depth) ** 0.25
    d_q = N_HEADS * QK_DIM
    d_kv = N_KV_HEADS * QK_DIM
    d_qkv = d_q + 2 * d_kv
    cd = p_conv.shape[1]
    q_scale = LOG2E * HEAD_DIM ** -0.5

    n_p = nb_p * seq_p
    n_s = nb_s * seq_s
    tm_p = _token_tile(seq_p, 512)
    tm_s = n_s
    tq = ATTN_BLOCK

    sigma = _alibi_sigma()
    rel = (jnp.arange(tq, dtype=F32)[None, :] - jnp.arange(tq, dtype=F32)[:, None])
    sig_gr = sigma.reshape(N_KV_HEADS, REP, 1, 1)
    bias_off = -sig_gr * rel
    bias_diag = jnp.where(rel >= 0, bias_off, NEG_BIG)

    n_cols = 2 * N_KV_HEADS * REP * seq_s
    sig_cols = jnp.broadcast_to(sigma.reshape(1, N_KV_HEADS, REP, 1), (2, N_KV_HEADS, REP, seq_s)).reshape(n_cols)
    t_cols = jnp.broadcast_to(jnp.arange(seq_s, dtype=F32), (2, N_KV_HEADS, REP, seq_s)).reshape(n_cols)
    pad_cols = LANES - n_cols
    sig_cols = jnp.pad(sig_cols, (0, pad_cols))
    t_cols = jnp.pad(t_cols, (0, pad_cols))
    g_cols = jnp.broadcast_to(jnp.arange(N_KV_HEADS).reshape(1, N_KV_HEADS, 1, 1),
                              (2, N_KV_HEADS, REP, seq_s)).reshape(n_cols)
    g_cols = jnp.pad(g_cols, (0, pad_cols))
    row_id = jnp.arange(page_size * N_KV_HEADS)[:, None]
    same_head = (row_id % N_KV_HEADS) == g_cols[None, :]
    key_row = (row_id // N_KV_HEADS).astype(F32)
    base_bias = jnp.where(same_head, -sig_cols[None, :] * (past_len + t_cols[None, :] - key_row), NEG_BIG)
    pagevec = (sig_cols * page_size)[None, :]
    new_id = jnp.arange(NEW_ROWS)[:, None]
    new_row = (new_id // N_KV_HEADS).astype(F32)
    new_ok = ((new_id % N_KV_HEADS) == g_cols[None, :]) & (new_row <= t_cols[None, :]) & (new_row < seq_s)
    bias_new = jnp.where(new_ok, -sig_cols[None, :] * (t_cols[None, :] - new_row), NEG_BIG)
    cache_k2 = cache_k.reshape(depth, n_pool, page_size * N_KV_HEADS, QK_DIM)
    cache_v2 = cache_v.reshape(depth, n_pool, page_size * N_KV_HEADS, V_DIM)

    ws_tril = jnp.tril(gmlp_ws)
    eye_s = jnp.eye(CHUNK // seq_s, dtype=F32)

    row2 = lambda a: a.reshape(1, -1)
    router_pad = jnp.pad(router, ((0, 0), (0, 0), (0, LANES - n_experts)))

    yp = x_prompt.reshape(n_p, d_model)
    ys = x_sample.reshape(n_s, d_model)
    outs = {k: [] for k in ("kp", "vp", "cp", "ks", "vs", "cs", "gs")}
    for l in range(depth):
        lam_init = 0.8 - 0.6 * math.exp(-0.3 * l)
        lp = lam_params[l].astype(F32)
        lam = (jnp.exp(jnp.sum(lp[0] * lp[1])) - jnp.exp(jnp.sum(lp[2] * lp[3])) + lam_init).reshape(1)
        out_scale = 1.0 - lam_init
        w_l = w_in[l].astype(BF16)
        w_qkv, w_br = w_l[:, :d_qkv], w_l[:, d_qkv:]
        pa, pg, pc, wo = (p_attn[l].astype(BF16), p_gmlp[l].astype(BF16),
                          p_conv[l].astype(BF16), w_o[l].astype(BF16))
        ws_p = ws_tril[l].astype(BF16)
        bs_p = jnp.broadcast_to(gmlp_bs[l][:, :, None], (GMLP_GROUPS, CHUNK, LANES))
        ws_s = jnp.einsum("ab,gij->gaibj", eye_s, ws_tril[l][:, :seq_s, :seq_s]).reshape(
            GMLP_GROUPS, CHUNK, CHUNK).astype(BF16)
        bs_s = jnp.broadcast_to(jnp.tile(gmlp_bs[l][:, :seq_s], (1, CHUNK // seq_s))[:, :, None],
                                (GMLP_GROUPS, CHUNK, LANES))

        def mixer(x, tm):
            if l % 2 == 0:
                i = l // 2
                return _dense_ffn(x, ffn_gate[i].astype(BF16), ffn_up[i].astype(BF16),
                                  ffn_down[i].astype(BF16), row2(ln2_g[l]), row2(ln2_b[l]),
                                  tm=tm, alpha=alpha)
            i = l // 2
            return _moe_ffn(x, router_pad[i], moe_gate[i].astype(BF16), moe_up[i].astype(BF16),
                            moe_down[i].astype(BF16), row2(ln2_g[l]), row2(ln2_b[l]),
                            tm=_token_tile(x.shape[0], MOE_TILE), alpha=alpha, n_experts=n_experts)

        q0, q1, kf, vf, kb, vt = _qkv_proj(yp, w_qkv, prompt=True, tm=tm_p, seq_len=seq_p, tk=tq,
                                            q_scale=q_scale)
        o = _prompt_attention(lam, sigma, q0, q1, kb, vt, bias_off, bias_diag,
                              subln_g[l].reshape(V_DIM, 1), seq_len=seq_p, out_scale=out_scale)
        part, ga, ztail = _branch_proj(yp, w_br, row2(gmlp_ln_g[l]), row2(gmlp_ln_b[l]), ws_p, bs_p,
                                       conv_w[l], pg, pc, None, prompt=True, tm=tm_p,
                                       seq_len=seq_p, period=1)
        x1 = _mix_proj(yp, o, ga, part, pa, wo, row2(ln1_g[l]), row2(ln1_b[l]), tm=tm_p, alpha=alpha)
        yp = mixer(x1, tm_p)
        outs["kp"].append(kf.reshape(nb_p, seq_p, N_KV_HEADS, QK_DIM))
        outs["vp"].append(vf.reshape(nb_p, seq_p, N_KV_HEADS, V_DIM))
        tiles_per_seq = seq_p // tm_p
        outs["cp"].append(ztail.reshape(nb_p, tiles_per_seq, SUBLANES, cd)[:, -1, SUBLANES - (CONV_WIDTH - 1):])

        qs, kfs, vfs = _qkv_proj(ys, w_qkv, prompt=False, tm=tm_s, seq_len=seq_s, tk=tq, q_scale=q_scale)
        q6 = qs.reshape(nb_s, seq_s, N_KV_HEADS, REP, 2, HEAD_DIM)
        eye_m = jnp.eye(2, dtype=F32)
        qbd = jnp.einsum("btgrmd,mn->bmdngrt", q6, eye_m).reshape(nb_s, QK_DIM, n_cols)
        qbd = jnp.pad(qbd, ((0, 0), (0, 0), (0, pad_cols))).astype(BF16)
        pad_new = ((0, 0), (0, NEW_ROWS - seq_s * N_KV_HEADS), (0, 0))
        knew = jnp.pad(kfs.reshape(nb_s, seq_s * N_KV_HEADS, QK_DIM), pad_new)
        vnew = jnp.pad(vfs.reshape(nb_s, seq_s * N_KV_HEADS, V_DIM), pad_new)
        o_s = _sample_attention(page_table, lam, qbd, cache_k2, cache_v2, l, base_bias, pagevec,
                                knew, vnew, bias_new, subln_g[l].reshape(1, V_DIM), seq_s=seq_s,
                                out_scale=out_scale)
        o_s = o_s.reshape(nb_s, N_KV_HEADS, REP, seq_s, V_DIM).transpose(0, 3, 1, 2, 4).reshape(n_s, N_HEADS * V_DIM)
        prev = state_conv[l]
        zeros = jnp.zeros((nb_s, seq_s - 1, cd), F32)
        fix1 = jnp.concatenate([prev[:, 1:2], zeros], axis=1).reshape(n_s, cd)
        fix2 = jnp.concatenate([prev, zeros[:, 1:]], axis=1).reshape(n_s, cd)
        part_s, ga_s, z_s, zv_s = _branch_proj(ys, w_br, row2(gmlp_ln_g[l]), row2(gmlp_ln_b[l]), ws_s,
                                               bs_s, conv_w[l], pg, pc, (fix1, fix2), prompt=False,
                                               tm=tm_s, seq_len=seq_s, period=seq_s)
        x1s = _mix_proj(ys, o_s, ga_s, part_s, pa, wo, row2(ln1_g[l]), row2(ln1_b[l]), tm=tm_s, alpha=alpha)
        ys = mixer(x1s, tm_s)
        outs["ks"].append(kfs.reshape(nb_s, seq_s, N_KV_HEADS, QK_DIM))
        outs["vs"].append(vfs.reshape(nb_s, seq_s, N_KV_HEADS, V_DIM))
        outs["cs"].append(z_s.reshape(nb_s, seq_s, cd)[:, seq_s - (CONV_WIDTH - 1):])
        outs["gs"].append(zv_s.reshape(nb_s, seq_s, -1))

    return (yp.reshape(nb_p, seq_p, d_model), ys.reshape(nb_s, seq_s, d_model),
            jnp.stack(outs["kp"]), jnp.stack(outs["vp"]), jnp.stack(outs["cp"]),
            jnp.stack(outs["ks"]), jnp.stack(outs["vs"]), jnp.stack(outs["cs"]), jnp.stack(outs["gs"]))
```

```python
import functools
import math

import jax
import jax.numpy as jnp
from jax import lax
from jax.experimental import pallas as pl
from jax.experimental.pallas import tpu as pltpu

F32 = jnp.float32
BF16 = jnp.bfloat16

N_HEADS = 8
N_KV_HEADS = 4
REP = N_HEADS // N_KV_HEADS
HEAD_DIM = 64
QK_DIM = 2 * HEAD_DIM
V_DIM = 2 * HEAD_DIM
GMLP_GROUPS = 4
CHUNK = 128
CONV_WIDTH = 3
N_BRANCH = 3
LN_EPS = 1e-5
LOG2E = 1.4426950408889634
NEG_BIG = -1e30

LANES = 128
SUBLANES = 8
VMEM_LIMIT = 56 * 1024 * 1024

ATTN_BLOCK = 256
ATTN_UNROLL = 4
PAGES_PER_STEP = 8
MOE_TILE = 1024
ONES_ROWS = 16
NEW_ROWS = 16


def _cparams(sem):
    return pltpu.CompilerParams(dimension_semantics=sem, vmem_limit_bytes=VMEM_LIMIT)


def _resident(shape):
    nd = len(shape)
    return pl.BlockSpec(shape, lambda *_: (0,) * nd, pipeline_mode=pl.Buffered(1))


def _layer_norm(v, g, b):
    mu = jnp.mean(v, axis=-1, keepdims=True)
    d = v - mu
    var = jnp.mean(d * d, axis=-1, keepdims=True)
    return d * lax.rsqrt(var + LN_EPS) * g + b


def _dot(a, b):
    return jnp.dot(a, b, preferred_element_type=F32)


def _qkv_kernel(x_ref, w_ref, *out_refs, prompt, tk, q_scale):
    d_q = N_HEADS * QK_DIM
    d_k = N_KV_HEADS * QK_DIM
    xb = x_ref[...].astype(BF16)
    q = _dot(xb, w_ref[:, 0:d_q]) * q_scale
    k = _dot(xb, w_ref[:, d_q:d_q + d_k])
    v = _dot(xb, w_ref[:, d_q + d_k:])
    if prompt:
        q0_ref, q1_ref, kf_ref, vf_ref, kb_ref, vt_ref = out_refs
        first_half = (lax.broadcasted_iota(jnp.int32, q.shape, 1) % QK_DIM) < HEAD_DIM
        q0_ref[...] = jnp.where(first_half, q, 0.0).astype(BF16)
        q1_ref[...] = jnp.where(first_half, 0.0, q).astype(BF16)
        kf_ref[...] = k
        vf_ref[...] = v
        kb_ref[...] = k.astype(BF16)
        for g in range(N_KV_HEADS):
            for j in range(v.shape[0] // tk):
                blk = v[j * tk:(j + 1) * tk, g * V_DIM:(g + 1) * V_DIM]
                ones = jnp.ones((ONES_ROWS, tk), F32)
                vt_ref[g, j] = jnp.concatenate([blk.T, ones], axis=0).astype(BF16)
    else:
        q_ref, kf_ref, vf_ref = out_refs
        q_ref[...] = q
        kf_ref[...] = k
        vf_ref[...] = v


def _qkv_proj(x, w_qkv, *, prompt, tm, seq_len, tk, q_scale):
    n, d = x.shape
    d_q = N_HEADS * QK_DIM
    d_kv = N_KV_HEADS * QK_DIM
    nt = n // tm
    row = lambda i: (i, 0)
    in_specs = [pl.BlockSpec((tm, d), row), _resident(w_qkv.shape)]
    if prompt:
        tps = seq_len // tm
        nb = n // seq_len
        out_shape = [
            jax.ShapeDtypeStruct((n, d_q), BF16), jax.ShapeDtypeStruct((n, d_q), BF16),
            jax.ShapeDtypeStruct((n, d_kv), F32), jax.ShapeDtypeStruct((n, d_kv), F32),
            jax.ShapeDtypeStruct((n, d_kv), BF16),
            jax.ShapeDtypeStruct((nb, N_KV_HEADS, seq_len // tk, V_DIM + ONES_ROWS, tk), BF16),
        ]
        out_specs = [
            pl.BlockSpec((tm, d_q), row), pl.BlockSpec((tm, d_q), row),
            pl.BlockSpec((tm, d_kv), row), pl.BlockSpec((tm, d_kv), row),
            pl.BlockSpec((tm, d_kv), row),
            pl.BlockSpec((None, N_KV_HEADS, tm // tk, V_DIM + ONES_ROWS, tk),
                         lambda i: (i // tps, 0, i % tps, 0, 0)),
        ]
    else:
        out_shape = [jax.ShapeDtypeStruct((n, d_q), F32),
                     jax.ShapeDtypeStruct((n, d_kv), F32), jax.ShapeDtypeStruct((n, d_kv), F32)]
        out_specs = [pl.BlockSpec((tm, d_q), row), pl.BlockSpec((tm, d_kv), row),
                     pl.BlockSpec((tm, d_kv), row)]
    return pl.pallas_call(
        functools.partial(_qkv_kernel, prompt=prompt, tk=tk, q_scale=q_scale),
        grid=(nt,), in_specs=in_specs, out_specs=out_specs, out_shape=out_shape,
        compiler_params=_cparams(("arbitrary",)), name="qkv_proj",
    )(x, w_qkv)


def _prompt_attn_kernel(lam_ref, sig_ref, q0_ref, q1_ref, k_ref, vt_ref, boff_ref, bdiag_ref,
                        gain_ref, o_ref, acc_ref, s_ref, *, tq, tk, out_scale):
    g = pl.program_id(1)
    qi = pl.program_id(2)
    n_state = REP * 2

    def scores(j, h):
        r, m = divmod(h, 2)
        kj = k_ref[pl.ds(pl.multiple_of(j * tk, tk), tk), :]
        qh = (q0_ref if m == 0 else q1_ref)[:, r * QK_DIM:(r + 1) * QK_DIM]
        return lax.dot_general(kj, qh, (((1,), (1,)), ((), ())), preferred_element_type=F32)

    def block(j, bias_ref, carry, prefetch):
        m_all = carry
        vtj = vt_ref[j]
        dist_blocks = (qi - j).astype(F32) * float(tq)
        m_out = []
        for h in range(n_state):
            r = h // 2
            c_j = -sig_ref[g * REP + r] * dist_blocks
            s_t = s_ref[h] + bias_ref[r]
            if prefetch:
                s_ref[h] = scores(j + 1, h)
            m_new = jnp.maximum(m_all[h], jnp.max(s_t, axis=0, keepdims=True) + c_j)
            alpha = jnp.exp2(m_all[h] - m_new)
            p_t = jnp.exp2(s_t - (m_new - c_j))
            m_out.append(m_new)
            acc_ref[h] = acc_ref[h] * alpha + _dot(vtj, p_t.astype(BF16))
        return tuple(m_out)

    acc_ref[...] = jnp.zeros_like(acc_ref)
    for h in range(n_state):
        s_ref[h] = scores(0, h)
    init = tuple(jnp.full((1, tq), NEG_BIG, F32) for _ in range(n_state))
    def group(jj, c):
        for u in range(ATTN_UNROLL):
            c = block(ATTN_UNROLL * jj + u, boff_ref, c, True)
        return c

    n_grouped = (qi // ATTN_UNROLL) * ATTN_UNROLL
    carry = lax.fori_loop(0, qi // ATTN_UNROLL, group, init)
    carry = lax.fori_loop(n_grouped, qi, lambda j, c: block(j, boff_ref, c, True), carry)
    block(qi, bdiag_ref, carry, False)

    lam = lam_ref[0]
    gain = gain_ref[...] * out_scale
    for r in range(REP):
        a0, a1 = acc_ref[2 * r], acc_ref[2 * r + 1]
        o_t = a0[:V_DIM] / a0[V_DIM:V_DIM + 1] - lam * (a1[:V_DIM] / a1[V_DIM:V_DIM + 1])
        ms = jnp.mean(o_t * o_t, axis=0, keepdims=True)
        o_t = o_t * lax.rsqrt(ms + LN_EPS) * gain
        o_ref[:, r * V_DIM:(r + 1) * V_DIM] = o_t.T.astype(o_ref.dtype)


def _prompt_attention(lam, sigma, q0, q1, kb, vt, bias_off, bias_diag, gain, *, seq_len, out_scale):
    n = q0.shape[0]
    nb = n // seq_len
    tq = tk = ATTN_BLOCK
    nq = seq_len // tq
    smem = pl.BlockSpec(memory_space=pltpu.SMEM)
    qspec = pl.BlockSpec((tq, REP * QK_DIM), lambda b, g, i: (b * nq + i, g))
    return pl.pallas_call(
        functools.partial(_prompt_attn_kernel, tq=tq, tk=tk, out_scale=out_scale),
        grid=(nb, N_KV_HEADS, nq),
        in_specs=[
            smem, smem, qspec, qspec,
            pl.BlockSpec((seq_len, QK_DIM), lambda b, g, i: (b, g)),
            pl.BlockSpec((None, None, seq_len // tk, V_DIM + ONES_ROWS, tk),
                         lambda b, g, i: (b, g, 0, 0, 0)),
            pl.BlockSpec((None, REP, tk, tq), lambda b, g, i: (g, 0, 0, 0)),
            pl.BlockSpec((None, REP, tk, tq), lambda b, g, i: (g, 0, 0, 0)),
            pl.BlockSpec((V_DIM, 1), lambda b, g, i: (0, 0)),
        ],
        out_specs=pl.BlockSpec((tq, REP * V_DIM), lambda b, g, i: (b * nq + i, g)),
        out_shape=jax.ShapeDtypeStruct((n, N_HEADS * V_DIM), BF16),
        scratch_shapes=[pltpu.VMEM((REP * 2, V_DIM + ONES_ROWS, tq), F32),
                        pltpu.VMEM((REP * 2, tk, tq), F32)],
        compiler_params=_cparams(("arbitrary", "arbitrary", "arbitrary")), name="prompt_attn",
    )(lam, sigma, q0, q1, kb, vt, bias_off, bias_diag, gain)


def _sample_attn_kernel(pt_ref, lam_ref, qbd_ref, *refs, n_pages_step, out_scale):
    del pt_ref
    P = n_pages_step
    k_refs = refs[:P]
    v_refs = refs[P:2 * P]
    (base_ref, pagevec_ref, knew_ref, vnew_ref, bnew_ref, gain_ref,
     o_ref, m_ref, l_ref, acc_ref, sa_ref, sb_ref) = refs[2 * P:]
    c = pl.program_id(1)
    n_c = pl.num_programs(1)
    rows_g = o_ref.shape[1]
    half = N_KV_HEADS * rows_g

    @pl.when(c == 0)
    def _():
        m_ref[...] = jnp.full_like(m_ref, NEG_BIG)
        l_ref[...] = jnp.zeros_like(l_ref)
        acc_ref[...] = jnp.zeros_like(acc_ref)
        sb_ref[...] = jnp.full_like(sb_ref, 2.0 * NEG_BIG)

    qbd = qbd_ref[...]

    def col_bcast(row):
        return jnp.transpose(jnp.broadcast_to(row, (LANES, LANES)))

    def update(s_tiles, v_tiles):
        m_old = m_ref[...]
        mx = s_tiles[0].max(axis=0, keepdims=True)
        for s_t in s_tiles[1:]:
            mx = jnp.maximum(mx, s_t.max(axis=0, keepdims=True))
        m_new = jnp.maximum(m_old, mx)
        alpha = jnp.exp2(m_old - m_new)
        lsum = jnp.zeros_like(m_old)
        pv = None
        for s_t, v_t in zip(s_tiles, v_tiles):
            p_t = jnp.exp2(s_t - m_new)
            lsum = lsum + jnp.sum(p_t, axis=0, keepdims=True)
            part = lax.dot_general(p_t.astype(BF16), v_t, (((0,), (0,)), ((), ())),
                                   preferred_element_type=F32)
            pv = part if pv is None else pv + part
        m_ref[...] = m_new
        l_ref[...] = alpha * l_ref[...] + lsum
        acc_ref[...] = acc_ref[...] * col_bcast(alpha) + pv

    def step(s_write, s_read):
        base = base_ref[...]
        page0 = (jnp.minimum(c, n_c - 2) * P).astype(F32)
        for p in range(P):
            s_write[p] = (_dot(k_refs[p][...].astype(BF16), qbd) + base
                          + pagevec_ref[...] * (page0 + float(p)))
        update([s_read[p] for p in range(P)], [v_refs[p][...].astype(BF16) for p in range(P)])

    @pl.when(c % 2 == 0)
    def _():
        step(sa_ref, sb_ref)

    @pl.when(c % 2 == 1)
    def _():
        step(sb_ref, sa_ref)

    @pl.when(c == n_c - 1)
    def _():
        s_new = _dot(knew_ref[...].astype(BF16), qbd) + bnew_ref[...]
        update([s_new], [vnew_ref[...].astype(BF16)])
        l_col = col_bcast(l_ref[...])
        acc = acc_ref[...]
        o = acc[:half] / l_col[:half] - lam_ref[0] * (acc[half:2 * half] / l_col[half:2 * half])
        ms = jnp.mean(o * o, axis=-1, keepdims=True)
        o = o * lax.rsqrt(ms + LN_EPS) * (gain_ref[...] * out_scale)
        for g in range(N_KV_HEADS):
            o_ref[g] = o[g * rows_g:(g + 1) * rows_g].astype(o_ref.dtype)


def _sample_attention(page_table, lam, qbd, cache_k, cache_v, layer, base, pagevec, knew, vnew,
                      bias_new, gain, *, seq_s, out_scale):
    nb, n_pages = page_table.shape
    P = PAGES_PER_STEP
    page_shape = cache_k.shape[2:]
    rows_g = REP * seq_s
    smem = pl.BlockSpec(memory_space=pltpu.SMEM)

    n_chunks = n_pages // P

    def k_spec(p):
        return pl.BlockSpec((None, None) + page_shape,
                            lambda b, c, pt: (layer, pt[b, jnp.minimum(c, n_chunks - 1) * P + p], 0, 0))

    def v_spec(p):
        return pl.BlockSpec((None, None) + page_shape,
                            lambda b, c, pt: (layer, pt[b, jnp.maximum(c - 1, 0) * P + p], 0, 0))

    const2 = lambda b, c, pt: (0, 0)
    per_seq = lambda b, c, pt: (b, 0, 0)
    grid_spec = pltpu.PrefetchScalarGridSpec(
        num_scalar_prefetch=1,
        grid=(nb, n_chunks + 1),
        in_specs=[smem, pl.BlockSpec((None,) + qbd.shape[1:], per_seq)]
        + [k_spec(p) for p in range(P)] + [v_spec(p) for p in range(P)]
        + [pl.BlockSpec(base.shape, const2), pl.BlockSpec(pagevec.shape, const2),
           pl.BlockSpec((None,) + knew.shape[1:], per_seq),
           pl.BlockSpec((None,) + vnew.shape[1:], per_seq),
           pl.BlockSpec(bias_new.shape, const2), pl.BlockSpec(gain.shape, const2)],
        out_specs=pl.BlockSpec((None, N_KV_HEADS, rows_g, V_DIM), lambda b, c, pt: (b, 0, 0, 0)),
        scratch_shapes=[pltpu.VMEM((1, LANES), F32), pltpu.VMEM((1, LANES), F32),
                        pltpu.VMEM((LANES, V_DIM), F32),
                        pltpu.VMEM((P, page_shape[0], LANES), F32),
                        pltpu.VMEM((P, page_shape[0], LANES), F32)],
    )
    return pl.pallas_call(
        functools.partial(_sample_attn_kernel, n_pages_step=P, out_scale=out_scale),
        grid_spec=grid_spec,
        out_shape=jax.ShapeDtypeStruct((nb, N_KV_HEADS, rows_g, V_DIM), BF16),
        compiler_params=_cparams(("arbitrary", "arbitrary")), name="sample_attn",
    )(page_table, lam, qbd, *([cache_k] * P), *([cache_v] * P), base, pagevec, knew, vnew,
      bias_new, gain)


def _branch_kernel(x_ref, w_ref, lng_ref, lnb_ref, ws_ref, bs_ref, cw_ref, pg_ref, pc_ref, *refs,
                   prompt, tiles_per_seq, period):
    d_model = x_ref.shape[1]
    gd = pg_ref.shape[0]
    cd = pc_ref.shape[0]
    tm = x_ref.shape[0]
    if prompt:
        part_ref, ga_ref, ztail_ref, carry_ref = refs
    else:
        fix1_ref, fix2_ref, part_ref, ga_ref, z_ref, zv_ref = refs
    xb = x_ref[...].astype(BF16)

    def proj(lo, width):
        return _dot(xb, w_ref[:, lo:lo + width])

    z_u = jax.nn.gelu(proj(0, gd))
    z_v = _layer_norm(jax.nn.gelu(proj(gd, gd)), lng_ref[...], lnb_ref[...])
    if not prompt:
        zv_ref[...] = z_v
    gw = gd // GMLP_GROUPS
    z_vb = z_v.astype(BF16)
    rows = []
    for n in range(tm // CHUNK):
        cols = []
        for g in range(GMLP_GROUPS):
            blk = z_vb[n * CHUNK:(n + 1) * CHUNK, g * gw:(g + 1) * gw]
            cols.append(_dot(ws_ref[g], blk) + bs_ref[g])
        rows.append(jnp.concatenate(cols, axis=1))
    s = jnp.concatenate(rows, axis=0)
    y_g = _dot((z_u * s).astype(BF16), pg_ref[...])

    off = 2 * gd
    c_b = proj(off, cd)
    z = proj(off + cd, cd) * proj(off + 2 * cd, cd)
    zr1 = pltpu.roll(z, 1, 0)
    zr2 = pltpu.roll(z, 2, 0)
    if prompt:
        @pl.when(pl.program_id(0) % tiles_per_seq == 0)
        def _():
            carry_ref[...] = jnp.zeros_like(carry_ref)

        prev = carry_ref[...]
        r8 = lax.broadcasted_iota(jnp.int32, (SUBLANES, cd), 0)
        top1 = jnp.where(r8 < 1, pltpu.roll(prev, 1, 0), zr1[:SUBLANES])
        top2 = jnp.where(r8 < 2, pltpu.roll(prev, 2, 0), zr2[:SUBLANES])
        zm1 = jnp.concatenate([top1, zr1[SUBLANES:]], axis=0)
        zm2 = jnp.concatenate([top2, zr2[SUBLANES:]], axis=0)
        tail = z[tm - SUBLANES:]
        carry_ref[...] = tail
        ztail_ref[...] = tail
    else:
        pos = lax.broadcasted_iota(jnp.int32, z.shape, 0) % period
        zm1 = jnp.where(pos >= 1, zr1, fix1_ref[...])
        zm2 = jnp.where(pos >= 2, zr2, fix2_ref[...])
        z_ref[...] = z
    cw = cw_ref[...]
    y = cw[0:1] * zm2 + cw[1:2] * zm1 + cw[2:3] * z
    y_c = _dot((c_b * y).astype(BF16), pc_ref[...])

    off = 2 * gd + 3 * cd
    ga_ref[...] = jax.nn.sigmoid(proj(off, d_model)).astype(ga_ref.dtype)
    part_ref[...] = (jax.nn.sigmoid(proj(off + d_model, d_model)) * y_g
                     + jax.nn.sigmoid(proj(off + 2 * d_model, d_model)) * y_c)


def _branch_proj(x, w_br, ln_g, ln_b, ws_mat, bs_mat, conv_w, p_gmlp, p_conv, fixes, *,
                 prompt, tm, seq_len, period):
    n, d = x.shape
    gd, cd = p_gmlp.shape[0], p_conv.shape[0]
    nt = n // tm
    row = lambda i: (i, 0)
    in_specs = [pl.BlockSpec((tm, d), row), _resident(w_br.shape), _resident(ln_g.shape),
                _resident(ln_b.shape), _resident(ws_mat.shape), _resident(bs_mat.shape),
                _resident(conv_w.shape), _resident(p_gmlp.shape), _resident(p_conv.shape)]
    args = [x, w_br, ln_g, ln_b, ws_mat, bs_mat, conv_w, p_gmlp, p_conv]
    out_shape = [jax.ShapeDtypeStruct((n, d), F32), jax.ShapeDtypeStruct((n, d), BF16)]
    out_specs = [pl.BlockSpec((tm, d), row), pl.BlockSpec((tm, d), row)]
    scratch = []
    if prompt:
        out_shape.append(jax.ShapeDtypeStruct((nt * SUBLANES, cd), F32))
        out_specs.append(pl.BlockSpec((SUBLANES, cd), row))
        scratch.append(pltpu.VMEM((SUBLANES, cd), F32))
    else:
        in_specs += [pl.BlockSpec((tm, cd), row), pl.BlockSpec((tm, cd), row)]
        args += list(fixes)
        out_shape += [jax.ShapeDtypeStruct((n, cd), F32), jax.ShapeDtypeStruct((n, gd), F32)]
        out_specs += [pl.BlockSpec((tm, cd), row), pl.BlockSpec((tm, gd), row)]
    return pl.pallas_call(
        functools.partial(_branch_kernel, prompt=prompt,
                          tiles_per_seq=max(seq_len // tm, 1), period=period),
        grid=(nt,), in_specs=in_specs, out_specs=out_specs, out_shape=out_shape,
        scratch_shapes=scratch,
        compiler_params=_cparams(("arbitrary",)), name="branch_proj",
    )(*args)


def _mix_kernel(x_ref, o_ref, ga_ref, part_ref, pa_ref, wo_ref, g_ref, b_ref, out_ref, *, alpha):
    y_a = _dot(o_ref[...], pa_ref[...])
    mix = ga_ref[...].astype(F32) * y_a + part_ref[...]
    h = _dot(mix.astype(BF16), wo_ref[...])
    out_ref[...] = _layer_norm(alpha * x_ref[...] + h, g_ref[...], b_ref[...])


def _mix_proj(x, o, ga, part, p_attn, w_o, ln_g, ln_b, *, tm, alpha):
    n, d = x.shape
    row = lambda i: (i, 0)
    return pl.pallas_call(
        functools.partial(_mix_kernel, alpha=alpha),
        grid=(n // tm,),
        in_specs=[pl.BlockSpec((tm, d), row), pl.BlockSpec((tm, o.shape[1]), row),
                  pl.BlockSpec((tm, d), row), pl.BlockSpec((tm, d), row),
                  _resident(p_attn.shape), _resident(w_o.shape),
                  _resident(ln_g.shape), _resident(ln_b.shape)],
        out_specs=pl.BlockSpec((tm, d), row),
        out_shape=jax.ShapeDtypeStruct((n, d), F32),
        compiler_params=_cparams(("arbitrary",)), name="mix_proj",
    )(x, o, ga, part, p_attn, w_o, ln_g, ln_b)


def _ffn_kernel(x_ref, wg_ref, wu_ref, wd_ref, g_ref, b_ref, out_ref, *, alpha, ff_chunk):
    x = x_ref[...]
    xb = x.astype(BF16)
    f = None
    for c0 in range(0, wg_ref.shape[1], ff_chunk):
        h = jax.nn.silu(_dot(xb, wg_ref[:, c0:c0 + ff_chunk])) * _dot(xb, wu_ref[:, c0:c0 + ff_chunk])
        part = _dot(h.astype(BF16), wd_ref[c0:c0 + ff_chunk, :])
        f = part if f is None else f + part
    out_ref[...] = _layer_norm(alpha * x + f, g_ref[...], b_ref[...])


def _ff_chunk(d_ff):
    for c in (1408, 1024, 512, 256, 128):
        if d_ff % c == 0:
            return c
    return d_ff


def _dense_ffn(x, wg, wu, wd, ln_g, ln_b, *, tm, alpha):
    n, d = x.shape
    row = lambda i: (i, 0)
    return pl.pallas_call(
        functools.partial(_ffn_kernel, alpha=alpha, ff_chunk=_ff_chunk(wg.shape[1])),
        grid=(n // tm,),
        in_specs=[pl.BlockSpec((tm, d), row), _resident(wg.shape), _resident(wu.shape),
                  _resident(wd.shape), _resident(ln_g.shape), _resident(ln_b.shape)],
        out_specs=pl.BlockSpec((tm, d), row),
        out_shape=jax.ShapeDtypeStruct((n, d), F32),
        compiler_params=_cparams(("arbitrary",)), name="dense_ffn",
    )(x, wg, wu, wd, ln_g, ln_b)


def _router_kernel(x_ref, r_ref, tri_ref, comb_ref, rank_ref, cnt_ref, *, n_experts):
    logits = jnp.dot(x_ref[...], r_ref[...], preferred_element_type=F32,
                     precision=lax.Precision.HIGHEST)
    lt = logits.T[:n_experts]
    row = lax.broadcasted_iota(jnp.int32, lt.shape, 0)
    m1 = jnp.max(lt, axis=0, keepdims=True)
    i1 = jnp.min(jnp.where(lt == m1, row, n_experts), axis=0, keepdims=True)
    rest = jnp.where(row == i1, -jnp.inf, lt)
    m2 = jnp.max(rest, axis=0, keepdims=True)
    i2 = jnp.min(jnp.where(rest == m2, row, n_experts), axis=0, keepdims=True)
    t = jnp.exp(m2 - m1)
    w1 = 1.0 / (1.0 + t)
    comb = jnp.where(row == i1, w1, 0.0) + jnp.where(row == i2, t * w1, 0.0)
    routed = jnp.where(comb > 0.0, 1.0, 0.0)
    lhs = jnp.concatenate([routed, jnp.zeros_like(routed)], axis=0).astype(BF16)
    comb_ref[...] = comb
    rank_ref[...] = _dot(lhs, tri_ref[...])[:n_experts]
    cnt_ref[...] = jnp.broadcast_to(jnp.sum(routed, axis=1, keepdims=True), cnt_ref.shape)


def _moe_kernel(cnt_ref, x_ref, comb_ref, rank_ref, wg_ref, wu_ref, wd_ref, g_ref, b_ref, out_ref,
                xb_ref, xs_ref, y_ref, *, alpha, rows):
    i = pl.program_id(0)
    e = pl.program_id(1)
    c = pl.program_id(2)
    last_c = pl.num_programs(2) - 1
    t = x_ref.shape[0]
    n_blk = (cnt_ref[i, e] + rows - 1) // rows
    comb_row = comb_ref[pl.ds(e, 1), :]
    rank_row = rank_ref[pl.ds(e, 1), :]

    @pl.when((e == 0) & (c == 0))
    def _():
        xb_ref[...] = x_ref[...].astype(BF16)
        out_ref[...] = jnp.zeros_like(out_ref)

    def block_rows(b):
        return pl.ds(pl.multiple_of(b * rows, rows), rows)

    def selection(b, weighted):
        slot = (lax.broadcasted_iota(jnp.int32, (rows, t), 0) + b * rows).astype(F32)
        hit = (rank_row == slot) & (comb_row > 0.0)
        return jnp.where(hit, comb_row if weighted else 1.0, 0.0).astype(BF16)

    def expert(b):
        xs = xs_ref[block_rows(b), :]
        h = jax.nn.silu(_dot(xs, wg_ref[...])) * _dot(xs, wu_ref[...])
        return _dot(h.astype(BF16), wd_ref[...])

    def for_blocks(body):
        def step(b, carry):
            body(b)
            return carry
        lax.fori_loop(0, n_blk, step, 0)

    @pl.when(c == 0)
    def _():
        def gather_and_run(b):
            xs_ref[block_rows(b), :] = _dot(selection(b, False), xb_ref[...]).astype(BF16)
            y_ref[block_rows(b), :] = expert(b)
        for_blocks(gather_and_run)

    @pl.when(c > 0)
    def _():
        def run(b):
            y_ref[block_rows(b), :] += expert(b)
        for_blocks(run)

    @pl.when(c == last_c)
    def _():
        def scatter(b):
            out_ref[...] += lax.dot_general(selection(b, True), y_ref[block_rows(b), :].astype(BF16),
                                            (((0,), (0,)), ((), ())), preferred_element_type=F32)
        for_blocks(scatter)

    @pl.when((e == pl.num_programs(1) - 1) & (c == last_c))
    def _():
        out_ref[...] = _layer_norm(alpha * x_ref[...] + out_ref[...], g_ref[...], b_ref[...])


def _moe_block_rows(tm, n_experts, top_k=2):
    mean = tm * top_k // n_experts
    return min(tm, -(-(mean + 64) // 64) * 64)


def _moe_ffn(x, router_pad, wg, wu, wd, ln_g, ln_b, *, tm, alpha, n_experts):
    n, d = x.shape
    nt = n // tm
    d_ff = wg.shape[2]
    fc = _ff_chunk(d_ff)
    chunk_major = lambda w: w.reshape(n_experts, d, d_ff // fc, fc).transpose(0, 2, 1, 3).astype(BF16)
    wg, wu, wd = chunk_major(wg), chunk_major(wu), wd.astype(BF16)
    rows = _moe_block_rows(tm, n_experts)
    cap = -(-tm // rows) * rows
    tri = (jnp.arange(tm)[:, None] < jnp.arange(tm)[None, :]).astype(BF16)
    comb, rank, cnt = pl.pallas_call(
        functools.partial(_router_kernel, n_experts=n_experts),
        grid=(nt,),
        in_specs=[pl.BlockSpec((tm, d), lambda i: (i, 0)), _resident(router_pad.shape),
                  _resident(tri.shape)],
        out_specs=[pl.BlockSpec((n_experts, tm), lambda i: (0, i)),
                   pl.BlockSpec((n_experts, tm), lambda i: (0, i)),
                   pl.BlockSpec((None, n_experts, LANES), lambda i: (i, 0, 0))],
        out_shape=[jax.ShapeDtypeStruct((n_experts, n), F32), jax.ShapeDtypeStruct((n_experts, n), F32),
                   jax.ShapeDtypeStruct((nt, n_experts, LANES), F32)],
        compiler_params=_cparams(("arbitrary",)), name="moe_router",
    )(x, router_pad, tri)
    counts = cnt[:, :, 0].astype(jnp.int32)
    row = lambda i, e, c, cnt: (i, 0)
    col = lambda i, e, c, cnt: (0, i)
    const = lambda i, e, c, cnt: (0, 0)
    grid_spec = pltpu.PrefetchScalarGridSpec(
        num_scalar_prefetch=1,
        grid=(nt, n_experts, d_ff // fc),
        in_specs=[pl.BlockSpec((tm, d), row, pipeline_mode=pl.Buffered(1)),
                  pl.BlockSpec((n_experts, tm), col), pl.BlockSpec((n_experts, tm), col),
                  pl.BlockSpec((None, None, d, fc), lambda i, e, c, cnt: (e, c, 0, 0)),
                  pl.BlockSpec((None, None, d, fc), lambda i, e, c, cnt: (e, c, 0, 0)),
                  pl.BlockSpec((None, fc, d), lambda i, e, c, cnt: (e, c, 0)),
                  pl.BlockSpec(ln_g.shape, const), pl.BlockSpec(ln_b.shape, const)],
        out_specs=pl.BlockSpec((tm, d), row),
        scratch_shapes=[pltpu.VMEM((tm, d), BF16), pltpu.VMEM((cap, d), BF16),
                        pltpu.VMEM((cap, d), F32)],
    )
    return pl.pallas_call(
        functools.partial(_moe_kernel, alpha=alpha, rows=rows),
        grid_spec=grid_spec,
        out_shape=jax.ShapeDtypeStruct((n, d), F32),
        compiler_params=_cparams(("arbitrary", "arbitrary", "arbitrary")), name="moe_ffn",
    )(counts, x, comb, rank, wg, wu, wd, ln_g, ln_b)


def _token_tile(n, cap):
    t = cap
    while n % t:
        t //= 2
    return t


def _alibi_sigma():
    h = jnp.arange(1, N_HEADS + 1, dtype=F32)
    return jnp.exp2(-8.0 * h / N_HEADS) * LOG2E


def kernel(x_prompt, x_sample, cache_k, cache_v, state_conv, page_table, w_in, lam_params, subln_g, gmlp_ln_g, gmlp_ln_b, gmlp_ws, gmlp_bs, conv_w, p_attn, p_gmlp, p_conv, w_o, ln1_g, ln1_b, ln2_g, ln2_b, ffn_gate, ffn_up, ffn_down, router, moe_gate, moe_up, moe_down):
    depth = w_in.shape[0]
    nb_p, seq_p, d_model = x_prompt.shape
    nb_s, seq_s, _ = x_sample.shape
    n_pool, page_size = cache_k.shape[1], cache_k.shape[2]
    n_pages = page_table.shape[1]
    past_len = n_pages * page_size
    n_experts = router.shape[2]
    alpha = (2 * depth) ** 0.25
    d_q = N_HEADS * QK_DIM
    d_kv = N_KV_HEADS * QK_DIM
    d_qkv = d_q + 2 * d_kv
    cd = p_conv.shape[1]
    q_scale = LOG2E * HEAD_DIM ** -0.5

    n_p = nb_p * seq_p
    n_s = nb_s * seq_s
    tm_p = _token_tile(seq_p, 512)
    tm_s = n_s
    tq = ATTN_BLOCK

    sigma = _alibi_sigma()
    rel = (jnp.arange(tq, dtype=F32)[None, :] - jnp.arange(tq, dtype=F32)[:, None])
    sig_gr = sigma.reshape(N_KV_HEADS, REP, 1, 1)
    bias_off = -sig_gr * rel
    bias_diag = jnp.where(rel >= 0, bias_off, NEG_BIG)

    n_cols = 2 * N_KV_HEADS * REP * seq_s
    sig_cols = jnp.broadcast_to(sigma.reshape(1, N_KV_HEADS, REP, 1), (2, N_KV_HEADS, REP, seq_s)).reshape(n_cols)
    t_cols = jnp.broadcast_to(jnp.arange(seq_s, dtype=F32), (2, N_KV_HEADS, REP, seq_s)).reshape(n_cols)
    pad_cols = LANES - n_cols
    sig_cols = jnp.pad(sig_cols, (0, pad_cols))
    t_cols = jnp.pad(t_cols, (0, pad_cols))
    g_cols = jnp.broadcast_to(jnp.arange(N_KV_HEADS).reshape(1, N_KV_HEADS, 1, 1),
                              (2, N_KV_HEADS, REP, seq_s)).reshape(n_cols)
    g_cols = jnp.pad(g_cols, (0, pad_cols))
    row_id = jnp.arange(page_size * N_KV_HEADS)[:, None]
    same_head = (row_id % N_KV_HEADS) == g_cols[None, :]
    key_row = (row_id // N_KV_HEADS).astype(F32)
    base_bias = jnp.where(same_head, -sig_cols[None, :] * (past_len + t_cols[None, :] - key_row), NEG_BIG)
    pagevec = (sig_cols * page_size)[None, :]
    new_id = jnp.arange(NEW_ROWS)[:, None]
    new_row = (new_id // N_KV_HEADS).astype(F32)
    new_ok = ((new_id % N_KV_HEADS) == g_cols[None, :]) & (new_row <= t_cols[None, :]) & (new_row < seq_s)
    bias_new = jnp.where(new_ok, -sig_cols[None, :] * (t_cols[None, :] - new_row), NEG_BIG)
    cache_k2 = cache_k.reshape(depth, n_pool, page_size * N_KV_HEADS, QK_DIM)
    cache_v2 = cache_v.reshape(depth, n_pool, page_size * N_KV_HEADS, V_DIM)

    ws_tril = jnp.tril(gmlp_ws)
    eye_s = jnp.eye(CHUNK // seq_s, dtype=F32)

    row2 = lambda a: a.reshape(1, -1)
    router_pad = jnp.pad(router, ((0, 0), (0, 0), (0, LANES - n_experts)))

    yp = x_prompt.reshape(n_p, d_model)
    ys = x_sample.reshape(n_s, d_model)
    outs = {k: [] for k in ("kp", "vp", "cp", "ks", "vs", "cs", "gs")}
    for l in range(depth):
        lam_init = 0.8 - 0.6 * math.exp(-0.3 * l)
        lp = lam_params[l].astype(F32)
        lam = (jnp.exp(jnp.sum(lp[0] * lp[1])) - jnp.exp(jnp.sum(lp[2] * lp[3])) + lam_init).reshape(1)
        out_scale = 1.0 - lam_init
        w_l = w_in[l].astype(BF16)
        w_qkv, w_br = w_l[:, :d_qkv], w_l[:, d_qkv:]
        pa, pg, pc, wo = (p_attn[l].astype(BF16), p_gmlp[l].astype(BF16),
                          p_conv[l].astype(BF16), w_o[l].astype(BF16))
        ws_p = ws_tril[l].astype(BF16)
        bs_p = jnp.broadcast_to(gmlp_bs[l][:, :, None], (GMLP_GROUPS, CHUNK, LANES))
        ws_s = jnp.einsum("ab,gij->gaibj", eye_s, ws_tril[l][:, :seq_s, :seq_s]).reshape(
            GMLP_GROUPS, CHUNK, CHUNK).astype(BF16)
        bs_s = jnp.broadcast_to(jnp.tile(gmlp_bs[l][:, :seq_s], (1, CHUNK // seq_s))[:, :, None],
                                (GMLP_GROUPS, CHUNK, LANES))

        def mixer(x, tm):
            if l % 2 == 0:
                i = l // 2
                return _dense_ffn(x, ffn_gate[i].astype(BF16), ffn_up[i].astype(BF16),
                                  ffn_down[i].astype(BF16), row2(ln2_g[l]), row2(ln2_b[l]),
                                  tm=tm, alpha=alpha)
            i = l // 2
            return _moe_ffn(x, router_pad[i], moe_gate[i], moe_up[i], moe_down[i],
                            row2(ln2_g[l]), row2(ln2_b[l]),
                            tm=_token_tile(x.shape[0], MOE_TILE), alpha=alpha, n_experts=n_experts)

        q0, q1, kf, vf, kb, vt = _qkv_proj(yp, w_qkv, prompt=True, tm=tm_p, seq_len=seq_p, tk=tq,
                                            q_scale=q_scale)
        o = _prompt_attention(lam, sigma, q0, q1, kb, vt, bias_off, bias_diag,
                              subln_g[l].reshape(V_DIM, 1), seq_len=seq_p, out_scale=out_scale)
        part, ga, ztail = _branch_proj(yp, w_br, row2(gmlp_ln_g[l]), row2(gmlp_ln_b[l]), ws_p, bs_p,
                                       conv_w[l], pg, pc, None, prompt=True, tm=tm_p,
                                       seq_len=seq_p, period=1)
        x1 = _mix_proj(yp, o, ga, part, pa, wo, row2(ln1_g[l]), row2(ln1_b[l]), tm=tm_p, alpha=alpha)
        yp = mixer(x1, tm_p)
        outs["kp"].append(kf.reshape(nb_p, seq_p, N_KV_HEADS, QK_DIM))
        outs["vp"].append(vf.reshape(nb_p, seq_p, N_KV_HEADS, V_DIM))
        tiles_per_seq = seq_p // tm_p
        outs["cp"].append(ztail.reshape(nb_p, tiles_per_seq, SUBLANES, cd)[:, -1, SUBLANES - (CONV_WIDTH - 1):])

        qs, kfs, vfs = _qkv_proj(ys, w_qkv, prompt=False, tm=tm_s, seq_len=seq_s, tk=tq, q_scale=q_scale)
        q6 = qs.reshape(nb_s, seq_s, N_KV_HEADS, REP, 2, HEAD_DIM)
        eye_m = jnp.eye(2, dtype=F32)
        qbd = jnp.einsum("btgrmd,mn->bmdngrt", q6, eye_m).reshape(nb_s, QK_DIM, n_cols)
        qbd = jnp.pad(qbd, ((0, 0), (0, 0), (0, pad_cols))).astype(BF16)
        pad_new = ((0, 0), (0, NEW_ROWS - seq_s * N_KV_HEADS), (0, 0))
        knew = jnp.pad(kfs.reshape(nb_s, seq_s * N_KV_HEADS, QK_DIM), pad_new)
        vnew = jnp.pad(vfs.reshape(nb_s, seq_s * N_KV_HEADS, V_DIM), pad_new)
        o_s = _sample_attention(page_table, lam, qbd, cache_k2, cache_v2, l, base_bias, pagevec,
                                knew, vnew, bias_new, subln_g[l].reshape(1, V_DIM), seq_s=seq_s,
                                out_scale=out_scale)
        o_s = o_s.reshape(nb_s, N_KV_HEADS, REP, seq_s, V_DIM).transpose(0, 3, 1, 2, 4).reshape(n_s, N_HEADS * V_DIM)
        prev = state_conv[l]
        zeros = jnp.zeros((nb_s, seq_s - 1, cd), F32)
        fix1 = jnp.concatenate([prev[:, 1:2], zeros], axis=1).reshape(n_s, cd)
        fix2 = jnp.concatenate([prev, zeros[:, 1:]], axis=1).reshape(n_s, cd)
        part_s, ga_s, z_s, zv_s = _branch_proj(ys, w_br, row2(gmlp_ln_g[l]), row2(gmlp_ln_b[l]), ws_s,
                                               bs_s, conv_w[l], pg, pc, (fix1, fix2), prompt=False,
                                               tm=tm_s, seq_len=seq_s, period=seq_s)
        x1s = _mix_proj(ys, o_s, ga_s, part_s, pa, wo, row2(ln1_g[l]), row2(ln1_b[l]), tm=tm_s, alpha=alpha)
        ys = mixer(x1s, tm_s)
        outs["ks"].append(kfs.reshape(nb_s, seq_s, N_KV_HEADS, QK_DIM))
        outs["vs"].append(vfs.reshape(nb_s, seq_s, N_KV_HEADS, V_DIM))
        outs["cs"].append(z_s.reshape(nb_s, seq_s, cd)[:, seq_s - (CONV_WIDTH - 1):])
        outs["gs"].append(zv_s.reshape(nb_s, seq_s, -1))

    return (yp.reshape(nb_p, seq_p, d_model), ys.reshape(nb_s, seq_s, d_model),
            jnp.stack(outs["kp"]), jnp.stack(outs["vp"]), jnp.stack(outs["cp"]),
            jnp.stack(outs["ks"]), jnp.stack(outs["vs"]), jnp.stack(outs["cs"]), jnp.stack(outs["gs"]))
```

```python
import functools
import math

import jax
import jax.numpy as jnp
from jax import lax
from jax.experimental import pallas as pl
from jax.experimental.pallas import tpu as pltpu

F32 = jnp.float32
BF16 = jnp.bfloat16

N_HEADS = 8
N_KV_HEADS = 4
REP = N_HEADS // N_KV_HEADS
HEAD_DIM = 64
QK_DIM = 2 * HEAD_DIM
V_DIM = 2 * HEAD_DIM
GMLP_GROUPS = 4
CHUNK = 128
CONV_WIDTH = 3
N_BRANCH = 3
LN_EPS = 1e-5
LOG2E = 1.4426950408889634
NEG_BIG = -1e30

LANES = 128
SUBLANES = 8
VMEM_LIMIT = 56 * 1024 * 1024

ATTN_BLOCK = 256
ATTN_UNROLL = 4
PAGES_PER_STEP = 16
MOE_TILE = 1024
ONES_ROWS = 16
NEW_ROWS = 16


def _cparams(sem):
    return pltpu.CompilerParams(dimension_semantics=sem, vmem_limit_bytes=VMEM_LIMIT)


def _resident(shape):
    nd = len(shape)
    return pl.BlockSpec(shape, lambda *_: (0,) * nd, pipeline_mode=pl.Buffered(1))


def _layer_norm(v, g, b):
    mu = jnp.mean(v, axis=-1, keepdims=True)
    d = v - mu
    var = jnp.mean(d * d, axis=-1, keepdims=True)
    return d * lax.rsqrt(var + LN_EPS) * g + b


def _dot(a, b):
    return jnp.dot(a, b, preferred_element_type=F32)


def _qkv_kernel(x_ref, w_ref, *refs, prompt, tk, q_scale, n_alias):
    out_refs = refs[n_alias:]
    d_q = N_HEADS * QK_DIM
    d_k = N_KV_HEADS * QK_DIM
    tm = x_ref.shape[0]
    xb = x_ref[...].astype(BF16)
    q = _dot(xb, w_ref[:, 0:d_q]) * q_scale
    k = _dot(xb, w_ref[:, d_q:d_q + d_k])
    v = _dot(xb, w_ref[:, d_q + d_k:])
    kf_ref, vf_ref = out_refs[-2:]
    for g in range(N_KV_HEADS):
        rows = pl.ds(g, tm, stride=N_KV_HEADS)
        kf_ref[rows, :] = k[:, g * QK_DIM:(g + 1) * QK_DIM]
        vf_ref[rows, :] = v[:, g * V_DIM:(g + 1) * V_DIM]
    if prompt:
        q0_ref, q1_ref, kb_ref, vt_ref = out_refs[:4]
        first_half = (lax.broadcasted_iota(jnp.int32, q.shape, 1) % QK_DIM) < HEAD_DIM
        q0_ref[...] = jnp.where(first_half, q, 0.0).astype(BF16)
        q1_ref[...] = jnp.where(first_half, 0.0, q).astype(BF16)
        kb_ref[...] = k.astype(BF16)
        for g in range(N_KV_HEADS):
            for j in range(tm // tk):
                blk = v[j * tk:(j + 1) * tk, g * V_DIM:(g + 1) * V_DIM]
                ones = jnp.ones((ONES_ROWS, tk), F32)
                vt_ref[g, j] = jnp.concatenate([blk.T, ones], axis=0).astype(BF16)
    else:
        out_refs[0][...] = q


def _qkv_proj(x, w_qkv, kv_stacks, *, layer, depth, prompt, tm, seq_len, tk, q_scale):
    n, d = x.shape
    d_q = N_HEADS * QK_DIM
    d_kv = N_KV_HEADS * QK_DIM
    nt = n // tm
    row = lambda i: (i, 0)
    in_specs = [pl.BlockSpec((tm, d), row), _resident(w_qkv.shape)]
    if prompt:
        tps = seq_len // tm
        nb = n // seq_len
        out_shape = [
            jax.ShapeDtypeStruct((n, d_q), BF16), jax.ShapeDtypeStruct((n, d_q), BF16),
            jax.ShapeDtypeStruct((n, d_kv), BF16),
            jax.ShapeDtypeStruct((nb, N_KV_HEADS, seq_len // tk, V_DIM + ONES_ROWS, tk), BF16),
        ]
        out_specs = [
            pl.BlockSpec((tm, d_q), row), pl.BlockSpec((tm, d_q), row),
            pl.BlockSpec((tm, d_kv), row),
            pl.BlockSpec((None, N_KV_HEADS, tm // tk, V_DIM + ONES_ROWS, tk),
                         lambda i: (i // tps, 0, i % tps, 0, 0)),
        ]
    else:
        out_shape = [jax.ShapeDtypeStruct((n, d_q), F32)]
        out_specs = [pl.BlockSpec((tm, d_q), row)]
    stack_shape = jax.ShapeDtypeStruct((depth, n * N_KV_HEADS, QK_DIM), F32)
    stack_spec = pl.BlockSpec((None, tm * N_KV_HEADS, QK_DIM), lambda i: (layer, i, 0))
    n_out = len(out_shape)
    out_shape += [stack_shape, stack_shape]
    out_specs += [stack_spec, stack_spec]
    args = [x, w_qkv]
    aliases = {}
    if kv_stacks is not None:
        in_specs += [pl.BlockSpec(memory_space=pl.ANY)] * 2
        args += list(kv_stacks)
        aliases = {2: n_out, 3: n_out + 1}
    return pl.pallas_call(
        functools.partial(_qkv_kernel, prompt=prompt, tk=tk, q_scale=q_scale, n_alias=len(aliases)),
        grid=(nt,), in_specs=in_specs, out_specs=out_specs, out_shape=out_shape,
        input_output_aliases=aliases,
        compiler_params=_cparams(("arbitrary",)), name="qkv_proj",
    )(*args)


def _prompt_attn_kernel(lam_ref, sig_ref, q0_ref, q1_ref, k_ref, vt_ref, boff_ref, bdiag_ref,
                        gain_ref, o_ref, acc_ref, s_ref, *, tq, tk, out_scale):
    g = pl.program_id(1)
    qi = pl.program_id(2)
    n_state = REP * 2

    def scores(j, h):
        r, m = divmod(h, 2)
        kj = k_ref[pl.ds(pl.multiple_of(j * tk, tk), tk), :]
        qh = (q0_ref if m == 0 else q1_ref)[:, r * QK_DIM:(r + 1) * QK_DIM]
        return lax.dot_general(kj, qh, (((1,), (1,)), ((), ())), preferred_element_type=F32)

    def block(j, bias_ref, carry, prefetch):
        m_all = carry
        vtj = vt_ref[j]
        dist_blocks = (qi - j).astype(F32) * float(tq)
        m_out = []
        for h in range(n_state):
            r = h // 2
            c_j = -sig_ref[g * REP + r] * dist_blocks
            s_t = s_ref[h] + bias_ref[r]
            if prefetch:
                s_ref[h] = scores(j + 1, h)
            m_new = jnp.maximum(m_all[h], jnp.max(s_t, axis=0, keepdims=True) + c_j)
            alpha = jnp.exp2(m_all[h] - m_new)
            p_t = jnp.exp2(s_t - (m_new - c_j))
            m_out.append(m_new)
            acc_ref[h] = acc_ref[h] * alpha + _dot(vtj, p_t.astype(BF16))
        return tuple(m_out)

    acc_ref[...] = jnp.zeros_like(acc_ref)
    for h in range(n_state):
        s_ref[h] = scores(0, h)
    init = tuple(jnp.full((1, tq), NEG_BIG, F32) for _ in range(n_state))
    def group(jj, c):
        for u in range(ATTN_UNROLL):
            c = block(ATTN_UNROLL * jj + u, boff_ref, c, True)
        return c

    n_grouped = (qi // ATTN_UNROLL) * ATTN_UNROLL
    carry = lax.fori_loop(0, qi // ATTN_UNROLL, group, init)
    carry = lax.fori_loop(n_grouped, qi, lambda j, c: block(j, boff_ref, c, True), carry)
    block(qi, bdiag_ref, carry, False)

    lam = lam_ref[0]
    gain = gain_ref[...] * out_scale
    for r in range(REP):
        a0, a1 = acc_ref[2 * r], acc_ref[2 * r + 1]
        o_t = a0[:V_DIM] / a0[V_DIM:V_DIM + 1] - lam * (a1[:V_DIM] / a1[V_DIM:V_DIM + 1])
        ms = jnp.mean(o_t * o_t, axis=0, keepdims=True)
        o_t = o_t * lax.rsqrt(ms + LN_EPS) * gain
        o_ref[:, r * V_DIM:(r + 1) * V_DIM] = o_t.T.astype(o_ref.dtype)


def _prompt_attention(lam, sigma, q0, q1, kb, vt, bias_off, bias_diag, gain, *, seq_len, out_scale):
    n = q0.shape[0]
    nb = n // seq_len
    tq = tk = ATTN_BLOCK
    nq = seq_len // tq
    smem = pl.BlockSpec(memory_space=pltpu.SMEM)
    qspec = pl.BlockSpec((tq, REP * QK_DIM), lambda b, g, i: (b * nq + i, g))
    return pl.pallas_call(
        functools.partial(_prompt_attn_kernel, tq=tq, tk=tk, out_scale=out_scale),
        grid=(nb, N_KV_HEADS, nq),
        in_specs=[
            smem, smem, qspec, qspec,
            pl.BlockSpec((seq_len, QK_DIM), lambda b, g, i: (b, g)),
            pl.BlockSpec((None, None, seq_len // tk, V_DIM + ONES_ROWS, tk),
                         lambda b, g, i: (b, g, 0, 0, 0)),
            pl.BlockSpec((None, REP, tk, tq), lambda b, g, i: (g, 0, 0, 0)),
            pl.BlockSpec((None, REP, tk, tq), lambda b, g, i: (g, 0, 0, 0)),
            pl.BlockSpec((V_DIM, 1), lambda b, g, i: (0, 0)),
        ],
        out_specs=pl.BlockSpec((tq, REP * V_DIM), lambda b, g, i: (b * nq + i, g)),
        out_shape=jax.ShapeDtypeStruct((n, N_HEADS * V_DIM), BF16),
        scratch_shapes=[pltpu.VMEM((REP * 2, V_DIM + ONES_ROWS, tq), F32),
                        pltpu.VMEM((REP * 2, tk, tq), F32)],
        compiler_params=_cparams(("arbitrary", "arbitrary", "arbitrary")), name="prompt_attn",
    )(lam, sigma, q0, q1, kb, vt, bias_off, bias_diag, gain)


def _sample_attn_kernel(pt_ref, lam_ref, qbd_ref, *refs, n_pages_step, out_scale):
    del pt_ref
    P = n_pages_step
    k_refs = refs[:P]
    v_refs = refs[P:2 * P]
    (base_ref, pagevec_ref, knew_ref, vnew_ref, bnew_ref, gain_ref,
     o_ref, m_ref, l_ref, acc_ref, sa_ref, sb_ref) = refs[2 * P:]
    c = pl.program_id(1)
    n_c = pl.num_programs(1)
    rows_g = o_ref.shape[1]
    half = N_KV_HEADS * rows_g

    @pl.when(c == 0)
    def _():
        m_ref[...] = jnp.full_like(m_ref, NEG_BIG)
        l_ref[...] = jnp.zeros_like(l_ref)
        acc_ref[...] = jnp.zeros_like(acc_ref)
        sb_ref[...] = jnp.full_like(sb_ref, 2.0 * NEG_BIG)

    qbd = qbd_ref[...]

    def col_bcast(row):
        return jnp.transpose(jnp.broadcast_to(row, (LANES, LANES)))

    def update(s_tiles, v_tiles):
        m_old = m_ref[...]
        mx = s_tiles[0].max(axis=0, keepdims=True)
        for s_t in s_tiles[1:]:
            mx = jnp.maximum(mx, s_t.max(axis=0, keepdims=True))
        m_new = jnp.maximum(m_old, mx)
        alpha = jnp.exp2(m_old - m_new)
        lsum = jnp.zeros_like(m_old)
        pv = None
        for s_t, v_t in zip(s_tiles, v_tiles):
            p_t = jnp.exp2(s_t - m_new)
            lsum = lsum + jnp.sum(p_t, axis=0, keepdims=True)
            part = lax.dot_general(p_t.astype(BF16), v_t, (((0,), (0,)), ((), ())),
                                   preferred_element_type=F32)
            pv = part if pv is None else pv + part
        m_ref[...] = m_new
        l_ref[...] = alpha * l_ref[...] + lsum
        acc_ref[...] = acc_ref[...] * col_bcast(alpha) + pv

    def step(s_write, s_read):
        base = base_ref[...]
        page0 = (jnp.minimum(c, n_c - 2) * P).astype(F32)
        for p in range(P):
            s_write[p] = (_dot(k_refs[p][...].astype(BF16), qbd) + base
                          + pagevec_ref[...] * (page0 + float(p)))
        update([s_read[p] for p in range(P)], [v_refs[p][...].astype(BF16) for p in range(P)])

    @pl.when(c % 2 == 0)
    def _():
        step(sa_ref, sb_ref)

    @pl.when(c % 2 == 1)
    def _():
        step(sb_ref, sa_ref)

    @pl.when(c == n_c - 1)
    def _():
        s_new = _dot(knew_ref[...].astype(BF16), qbd) + bnew_ref[...]
        update([s_new], [vnew_ref[...].astype(BF16)])
        l_col = col_bcast(l_ref[...])
        acc = acc_ref[...]
        o = acc[:half] / l_col[:half] - lam_ref[0] * (acc[half:2 * half] / l_col[half:2 * half])
        ms = jnp.mean(o * o, axis=-1, keepdims=True)
        o = o * lax.rsqrt(ms + LN_EPS) * (gain_ref[...] * out_scale)
        for g in range(N_KV_HEADS):
            o_ref[g] = o[g * rows_g:(g + 1) * rows_g].astype(o_ref.dtype)


def _sample_attention(page_table, lam, qbd, cache_k, cache_v, layer, base, pagevec, knew, vnew,
                      bias_new, gain, *, seq_s, out_scale):
    nb, n_pages = page_table.shape
    P = PAGES_PER_STEP
    page_shape = cache_k.shape[2:]
    rows_g = REP * seq_s
    smem = pl.BlockSpec(memory_space=pltpu.SMEM)

    n_chunks = n_pages // P

    def k_spec(p):
        return pl.BlockSpec((None, None) + page_shape,
                            lambda b, c, pt: (layer, pt[b, jnp.minimum(c, n_chunks - 1) * P + p], 0, 0))

    def v_spec(p):
        return pl.BlockSpec((None, None) + page_shape,
                            lambda b, c, pt: (layer, pt[b, jnp.maximum(c - 1, 0) * P + p], 0, 0))

    const2 = lambda b, c, pt: (0, 0)
    per_seq = lambda b, c, pt: (b, 0, 0)
    grid_spec = pltpu.PrefetchScalarGridSpec(
        num_scalar_prefetch=1,
        grid=(nb, n_chunks + 1),
        in_specs=[smem, pl.BlockSpec((None,) + qbd.shape[1:], per_seq)]
        + [k_spec(p) for p in range(P)] + [v_spec(p) for p in range(P)]
        + [pl.BlockSpec(base.shape, const2), pl.BlockSpec(pagevec.shape, const2),
           pl.BlockSpec((None,) + knew.shape[1:], per_seq),
           pl.BlockSpec((None,) + vnew.shape[1:], per_seq),
           pl.BlockSpec(bias_new.shape, const2), pl.BlockSpec(gain.shape, const2)],
        out_specs=pl.BlockSpec((None, N_KV_HEADS, rows_g, V_DIM), lambda b, c, pt: (b, 0, 0, 0)),
        scratch_shapes=[pltpu.VMEM((1, LANES), F32), pltpu.VMEM((1, LANES), F32),
                        pltpu.VMEM((LANES, V_DIM), F32),
                        pltpu.VMEM((P, page_shape[0], LANES), F32),
                        pltpu.VMEM((P, page_shape[0], LANES), F32)],
    )
    return pl.pallas_call(
        functools.partial(_sample_attn_kernel, n_pages_step=P, out_scale=out_scale),
        grid_spec=grid_spec,
        out_shape=jax.ShapeDtypeStruct((nb, N_KV_HEADS, rows_g, V_DIM), BF16),
        compiler_params=_cparams(("arbitrary", "arbitrary")), name="sample_attn",
    )(page_table, lam, qbd, *([cache_k] * P), *([cache_v] * P), base, pagevec, knew, vnew,
      bias_new, gain)


def _branch_kernel(x_ref, w_ref, lng_ref, lnb_ref, ws_ref, bs_ref, cw_ref, pg_ref, pc_ref, *refs,
                   prompt, tiles_per_seq, period):
    d_model = x_ref.shape[1]
    gd = pg_ref.shape[0]
    cd = pc_ref.shape[0]
    tm = x_ref.shape[0]
    if prompt:
        part_ref, ga_ref, ztail_ref, carry_ref = refs
    else:
        fix1_ref, fix2_ref, part_ref, ga_ref, z_ref, zv_ref = refs
    xb = x_ref[...].astype(BF16)

    def proj(lo, width):
        return _dot(xb, w_ref[:, lo:lo + width])

    z_u = jax.nn.gelu(proj(0, gd))
    z_v = _layer_norm(jax.nn.gelu(proj(gd, gd)), lng_ref[...], lnb_ref[...])
    if not prompt:
        zv_ref[...] = z_v
    gw = gd // GMLP_GROUPS
    z_vb = z_v.astype(BF16)
    rows = []
    for n in range(tm // CHUNK):
        cols = []
        for g in range(GMLP_GROUPS):
            blk = z_vb[n * CHUNK:(n + 1) * CHUNK, g * gw:(g + 1) * gw]
            cols.append(_dot(ws_ref[g], blk) + bs_ref[g])
        rows.append(jnp.concatenate(cols, axis=1))
    s = jnp.concatenate(rows, axis=0)
    y_g = _dot((z_u * s).astype(BF16), pg_ref[...])

    off = 2 * gd
    c_b = proj(off, cd)
    z = proj(off + cd, cd) * proj(off + 2 * cd, cd)
    zr1 = pltpu.roll(z, 1, 0)
    zr2 = pltpu.roll(z, 2, 0)
    if prompt:
        @pl.when(pl.program_id(0) % tiles_per_seq == 0)
        def _():
            carry_ref[...] = jnp.zeros_like(carry_ref)

        prev = carry_ref[...]
        r8 = lax.broadcasted_iota(jnp.int32, (SUBLANES, cd), 0)
        top1 = jnp.where(r8 < 1, pltpu.roll(prev, 1, 0), zr1[:SUBLANES])
        top2 = jnp.where(r8 < 2, pltpu.roll(prev, 2, 0), zr2[:SUBLANES])
        zm1 = jnp.concatenate([top1, zr1[SUBLANES:]], axis=0)
        zm2 = jnp.concatenate([top2, zr2[SUBLANES:]], axis=0)
        tail = z[tm - SUBLANES:]
        carry_ref[...] = tail
        ztail_ref[...] = tail
    else:
        pos = lax.broadcasted_iota(jnp.int32, z.shape, 0) % period
        zm1 = jnp.where(pos >= 1, zr1, fix1_ref[...])
        zm2 = jnp.where(pos >= 2, zr2, fix2_ref[...])
        z_ref[...] = z
    cw = cw_ref[...]
    y = cw[0:1] * zm2 + cw[1:2] * zm1 + cw[2:3] * z
    y_c = _dot((c_b * y).astype(BF16), pc_ref[...])

    off = 2 * gd + 3 * cd
    ga_ref[...] = jax.nn.sigmoid(proj(off, d_model)).astype(ga_ref.dtype)
    part_ref[...] = (jax.nn.sigmoid(proj(off + d_model, d_model)) * y_g
                     + jax.nn.sigmoid(proj(off + 2 * d_model, d_model)) * y_c)


def _branch_proj(x, w_br, ln_g, ln_b, ws_mat, bs_mat, conv_w, p_gmlp, p_conv, fixes, *,
                 prompt, tm, seq_len, period):
    n, d = x.shape
    gd, cd = p_gmlp.shape[0], p_conv.shape[0]
    nt = n // tm
    row = lambda i: (i, 0)
    in_specs = [pl.BlockSpec((tm, d), row), _resident(w_br.shape), _resident(ln_g.shape),
                _resident(ln_b.shape), _resident(ws_mat.shape), _resident(bs_mat.shape),
                _resident(conv_w.shape), _resident(p_gmlp.shape), _resident(p_conv.shape)]
    args = [x, w_br, ln_g, ln_b, ws_mat, bs_mat, conv_w, p_gmlp, p_conv]
    out_shape = [jax.ShapeDtypeStruct((n, d), F32), jax.ShapeDtypeStruct((n, d), BF16)]
    out_specs = [pl.BlockSpec((tm, d), row), pl.BlockSpec((tm, d), row)]
    scratch = []
    if prompt:
        out_shape.append(jax.ShapeDtypeStruct((nt * SUBLANES, cd), F32))
        out_specs.append(pl.BlockSpec((SUBLANES, cd), row))
        scratch.append(pltpu.VMEM((SUBLANES, cd), F32))
    else:
        in_specs += [pl.BlockSpec((tm, cd), row), pl.BlockSpec((tm, cd), row)]
        args += list(fixes)
        out_shape += [jax.ShapeDtypeStruct((n, cd), F32), jax.ShapeDtypeStruct((n, gd), F32)]
        out_specs += [pl.BlockSpec((tm, cd), row), pl.BlockSpec((tm, gd), row)]
    return pl.pallas_call(
        functools.partial(_branch_kernel, prompt=prompt,
                          tiles_per_seq=max(seq_len // tm, 1), period=period),
        grid=(nt,), in_specs=in_specs, out_specs=out_specs, out_shape=out_shape,
        scratch_shapes=scratch,
        compiler_params=_cparams(("arbitrary",)), name="branch_proj",
    )(*args)


def _mix_kernel(x_ref, o_ref, ga_ref, part_ref, pa_ref, wo_ref, g_ref, b_ref, out_ref, *, alpha):
    y_a = _dot(o_ref[...], pa_ref[...])
    mix = ga_ref[...].astype(F32) * y_a + part_ref[...]
    h = _dot(mix.astype(BF16), wo_ref[...])
    out_ref[...] = _layer_norm(alpha * x_ref[...] + h, g_ref[...], b_ref[...])


def _mix_proj(x, o, ga, part, p_attn, w_o, ln_g, ln_b, *, tm, alpha):
    n, d = x.shape
    row = lambda i: (i, 0)
    return pl.pallas_call(
        functools.partial(_mix_kernel, alpha=alpha),
        grid=(n // tm,),
        in_specs=[pl.BlockSpec((tm, d), row), pl.BlockSpec((tm, o.shape[1]), row),
                  pl.BlockSpec((tm, d), row), pl.BlockSpec((tm, d), row),
                  _resident(p_attn.shape), _resident(w_o.shape),
                  _resident(ln_g.shape), _resident(ln_b.shape)],
        out_specs=pl.BlockSpec((tm, d), row),
        out_shape=jax.ShapeDtypeStruct((n, d), F32),
        compiler_params=_cparams(("arbitrary",)), name="mix_proj",
    )(x, o, ga, part, p_attn, w_o, ln_g, ln_b)


def _ffn_kernel(x_ref, wg_ref, wu_ref, wd_ref, g_ref, b_ref, out_ref, *, alpha, ff_chunk):
    x = x_ref[...]
    xb = x.astype(BF16)
    f = None
    for c0 in range(0, wg_ref.shape[1], ff_chunk):
        h = jax.nn.silu(_dot(xb, wg_ref[:, c0:c0 + ff_chunk])) * _dot(xb, wu_ref[:, c0:c0 + ff_chunk])
        part = _dot(h.astype(BF16), wd_ref[c0:c0 + ff_chunk, :])
        f = part if f is None else f + part
    out_ref[...] = _layer_norm(alpha * x + f, g_ref[...], b_ref[...])


def _ff_chunk(d_ff):
    for c in (1408, 1024, 512, 256, 128):
        if d_ff % c == 0:
            return c
    return d_ff


def _dense_ffn(x, wg, wu, wd, ln_g, ln_b, *, tm, alpha):
    n, d = x.shape
    row = lambda i: (i, 0)
    return pl.pallas_call(
        functools.partial(_ffn_kernel, alpha=alpha, ff_chunk=_ff_chunk(wg.shape[1])),
        grid=(n // tm,),
        in_specs=[pl.BlockSpec((tm, d), row), _resident(wg.shape), _resident(wu.shape),
                  _resident(wd.shape), _resident(ln_g.shape), _resident(ln_b.shape)],
        out_specs=pl.BlockSpec((tm, d), row),
        out_shape=jax.ShapeDtypeStruct((n, d), F32),
        compiler_params=_cparams(("arbitrary",)), name="dense_ffn",
    )(x, wg, wu, wd, ln_g, ln_b)


def _router_kernel(x_ref, r_ref, tri_ref, comb_ref, rank_ref, cnt_ref, *, n_experts):
    logits = jnp.dot(x_ref[...], r_ref[...], preferred_element_type=F32,
                     precision=lax.Precision.HIGHEST)
    lt = logits.T[:n_experts]
    row = lax.broadcasted_iota(jnp.int32, lt.shape, 0)
    m1 = jnp.max(lt, axis=0, keepdims=True)
    i1 = jnp.min(jnp.where(lt == m1, row, n_experts), axis=0, keepdims=True)
    rest = jnp.where(row == i1, -jnp.inf, lt)
    m2 = jnp.max(rest, axis=0, keepdims=True)
    i2 = jnp.min(jnp.where(rest == m2, row, n_experts), axis=0, keepdims=True)
    t = jnp.exp(m2 - m1)
    w1 = 1.0 / (1.0 + t)
    comb = jnp.where(row == i1, w1, 0.0) + jnp.where(row == i2, t * w1, 0.0)
    routed = jnp.where(comb > 0.0, 1.0, 0.0)
    lhs = jnp.concatenate([routed, jnp.zeros_like(routed)], axis=0).astype(BF16)
    comb_ref[...] = comb
    rank_ref[...] = _dot(lhs, tri_ref[...])[:n_experts]
    cnt_ref[...] = jnp.broadcast_to(jnp.sum(routed, axis=1, keepdims=True), cnt_ref.shape)


def _moe_kernel(cnt_ref, x_ref, comb_ref, rank_ref, wg_ref, wu_ref, wd_ref, g_ref, b_ref, out_ref,
                xb_ref, xs_ref, y_ref, *, alpha, rows):
    i = pl.program_id(0)
    e = pl.program_id(1)
    c = pl.program_id(2)
    last_c = pl.num_programs(2) - 1
    t = x_ref.shape[0]
    n_blk = (cnt_ref[i, e] + rows - 1) // rows
    comb_row = comb_ref[pl.ds(e, 1), :]
    rank_row = rank_ref[pl.ds(e, 1), :]

    @pl.when((e == 0) & (c == 0))
    def _():
        xb_ref[...] = x_ref[...].astype(BF16)
        out_ref[...] = jnp.zeros_like(out_ref)

    def block_rows(b):
        return pl.ds(pl.multiple_of(b * rows, rows), rows)

    def selection(b, weighted):
        slot = (lax.broadcasted_iota(jnp.int32, (rows, t), 0) + b * rows).astype(F32)
        hit = (rank_row == slot) & (comb_row > 0.0)
        return jnp.where(hit, comb_row if weighted else 1.0, 0.0).astype(BF16)

    def expert(b):
        xs = xs_ref[block_rows(b), :]
        h = jax.nn.silu(_dot(xs, wg_ref[...])) * _dot(xs, wu_ref[...])
        return _dot(h.astype(BF16), wd_ref[...])

    def for_blocks(body):
        def step(b, carry):
            body(b)
            return carry
        lax.fori_loop(0, n_blk, step, 0)

    @pl.when(c == 0)
    def _():
        def gather_and_run(b):
            xs_ref[block_rows(b), :] = _dot(selection(b, False), xb_ref[...]).astype(BF16)
            y_ref[block_rows(b), :] = expert(b)
        for_blocks(gather_and_run)

    @pl.when(c > 0)
    def _():
        def run(b):
            y_ref[block_rows(b), :] += expert(b)
        for_blocks(run)

    @pl.when(c == last_c)
    def _():
        def scatter(b):
            out_ref[...] += lax.dot_general(selection(b, True), y_ref[block_rows(b), :].astype(BF16),
                                            (((0,), (0,)), ((), ())), preferred_element_type=F32)
        for_blocks(scatter)

    @pl.when((e == pl.num_programs(1) - 1) & (c == last_c))
    def _():
        out_ref[...] = _layer_norm(alpha * x_ref[...] + out_ref[...], g_ref[...], b_ref[...])


def _moe_block_rows(tm, n_experts, top_k=2):
    mean = tm * top_k // n_experts
    return min(tm, -(-(mean + 32) // 32) * 32)


def _moe_ffn(x, router_pad, wg, wu, wd, ln_g, ln_b, *, tm, alpha, n_experts):
    n, d = x.shape
    nt = n // tm
    d_ff = wg.shape[2]
    fc = _ff_chunk(d_ff)
    wg, wu, wd = wg.astype(BF16), wu.astype(BF16), wd.astype(BF16)
    rows = _moe_block_rows(tm, n_experts)
    cap = -(-tm // rows) * rows
    tri = (jnp.arange(tm)[:, None] < jnp.arange(tm)[None, :]).astype(BF16)
    comb, rank, cnt = pl.pallas_call(
        functools.partial(_router_kernel, n_experts=n_experts),
        grid=(nt,),
        in_specs=[pl.BlockSpec((tm, d), lambda i: (i, 0)), _resident(router_pad.shape),
                  _resident(tri.shape)],
        out_specs=[pl.BlockSpec((n_experts, tm), lambda i: (0, i)),
                   pl.BlockSpec((n_experts, tm), lambda i: (0, i)),
                   pl.BlockSpec((None, n_experts, LANES), lambda i: (i, 0, 0))],
        out_shape=[jax.ShapeDtypeStruct((n_experts, n), F32), jax.ShapeDtypeStruct((n_experts, n), F32),
                   jax.ShapeDtypeStruct((nt, n_experts, LANES), F32)],
        compiler_params=_cparams(("arbitrary",)), name="moe_router",
    )(x, router_pad, tri)
    counts = cnt[:, :, 0].astype(jnp.int32)
    row = lambda i, e, c, cnt: (i, 0)
    col = lambda i, e, c, cnt: (0, i)
    const = lambda i, e, c, cnt: (0, 0)
    grid_spec = pltpu.PrefetchScalarGridSpec(
        num_scalar_prefetch=1,
        grid=(nt, n_experts, d_ff // fc),
        in_specs=[pl.BlockSpec((tm, d), row, pipeline_mode=pl.Buffered(1)),
                  pl.BlockSpec((n_experts, tm), col), pl.BlockSpec((n_experts, tm), col),
                  pl.BlockSpec((None, d, fc), lambda i, e, c, cnt: (e, 0, c)),
                  pl.BlockSpec((None, d, fc), lambda i, e, c, cnt: (e, 0, c)),
                  pl.BlockSpec((None, fc, d), lambda i, e, c, cnt: (e, c, 0)),
                  pl.BlockSpec(ln_g.shape, const), pl.BlockSpec(ln_b.shape, const)],
        out_specs=pl.BlockSpec((tm, d), row),
        scratch_shapes=[pltpu.VMEM((tm, d), BF16), pltpu.VMEM((cap, d), BF16),
                        pltpu.VMEM((cap, d), F32)],
    )
    return pl.pallas_call(
        functools.partial(_moe_kernel, alpha=alpha, rows=rows),
        grid_spec=grid_spec,
        out_shape=jax.ShapeDtypeStruct((n, d), F32),
        compiler_params=_cparams(("arbitrary", "arbitrary", "arbitrary")), name="moe_ffn",
    )(counts, x, comb, rank, wg, wu, wd, ln_g, ln_b)


def _token_tile(n, cap):
    t = cap
    while n % t:
        t //= 2
    return t


def _alibi_sigma():
    h = jnp.arange(1, N_HEADS + 1, dtype=F32)
    return jnp.exp2(-8.0 * h / N_HEADS) * LOG2E


def kernel(x_prompt, x_sample, cache_k, cache_v, state_conv, page_table, w_in, lam_params, subln_g, gmlp_ln_g, gmlp_ln_b, gmlp_ws, gmlp_bs, conv_w, p_attn, p_gmlp, p_conv, w_o, ln1_g, ln1_b, ln2_g, ln2_b, ffn_gate, ffn_up, ffn_down, router, moe_gate, moe_up, moe_down):
    depth = w_in.shape[0]
    nb_p, seq_p, d_model = x_prompt.shape
    nb_s, seq_s, _ = x_sample.shape
    n_pool, page_size = cache_k.shape[1], cache_k.shape[2]
    n_pages = page_table.shape[1]
    past_len = n_pages * page_size
    n_experts = router.shape[2]
    alpha = (2 * depth) ** 0.25
    d_q = N_HEADS * QK_DIM
    d_kv = N_KV_HEADS * QK_DIM
    d_qkv = d_q + 2 * d_kv
    cd = p_conv.shape[1]
    q_scale = LOG2E * HEAD_DIM ** -0.5

    n_p = nb_p * seq_p
    n_s = nb_s * seq_s
    tm_p = _token_tile(seq_p, 512)
    tm_s = n_s
    tq = ATTN_BLOCK

    sigma = _alibi_sigma()
    rel = (jnp.arange(tq, dtype=F32)[None, :] - jnp.arange(tq, dtype=F32)[:, None])
    sig_gr = sigma.reshape(N_KV_HEADS, REP, 1, 1)
    bias_off = -sig_gr * rel
    bias_diag = jnp.where(rel >= 0, bias_off, NEG_BIG)

    n_cols = 2 * N_KV_HEADS * REP * seq_s
    sig_cols = jnp.broadcast_to(sigma.reshape(1, N_KV_HEADS, REP, 1), (2, N_KV_HEADS, REP, seq_s)).reshape(n_cols)
    t_cols = jnp.broadcast_to(jnp.arange(seq_s, dtype=F32), (2, N_KV_HEADS, REP, seq_s)).reshape(n_cols)
    pad_cols = LANES - n_cols
    sig_cols = jnp.pad(sig_cols, (0, pad_cols))
    t_cols = jnp.pad(t_cols, (0, pad_cols))
    g_cols = jnp.broadcast_to(jnp.arange(N_KV_HEADS).reshape(1, N_KV_HEADS, 1, 1),
                              (2, N_KV_HEADS, REP, seq_s)).reshape(n_cols)
    g_cols = jnp.pad(g_cols, (0, pad_cols))
    row_id = jnp.arange(page_size * N_KV_HEADS)[:, None]
    same_head = (row_id % N_KV_HEADS) == g_cols[None, :]
    key_row = (row_id // N_KV_HEADS).astype(F32)
    base_bias = jnp.where(same_head, -sig_cols[None, :] * (past_len + t_cols[None, :] - key_row), NEG_BIG)
    pagevec = (sig_cols * page_size)[None, :]
    new_id = jnp.arange(NEW_ROWS)[:, None]
    new_row = (new_id // N_KV_HEADS).astype(F32)
    new_ok = ((new_id % N_KV_HEADS) == g_cols[None, :]) & (new_row <= t_cols[None, :]) & (new_row < seq_s)
    bias_new = jnp.where(new_ok, -sig_cols[None, :] * (t_cols[None, :] - new_row), NEG_BIG)
    cache_k2 = cache_k.reshape(depth, n_pool, page_size * N_KV_HEADS, QK_DIM)
    cache_v2 = cache_v.reshape(depth, n_pool, page_size * N_KV_HEADS, V_DIM)

    ws_tril = jnp.tril(gmlp_ws)
    eye_s = jnp.eye(CHUNK // seq_s, dtype=F32)

    row2 = lambda a: a.reshape(1, -1)
    router_pad = jnp.pad(router, ((0, 0), (0, 0), (0, LANES - n_experts)))

    yp = x_prompt.reshape(n_p, d_model)
    ys = x_sample.reshape(n_s, d_model)
    outs = {k: [] for k in ("cp", "cs", "gs")}
    kv_p = kv_s = None
    for l in range(depth):
        lam_init = 0.8 - 0.6 * math.exp(-0.3 * l)
        lp = lam_params[l].astype(F32)
        lam = (jnp.exp(jnp.sum(lp[0] * lp[1])) - jnp.exp(jnp.sum(lp[2] * lp[3])) + lam_init).reshape(1)
        out_scale = 1.0 - lam_init
        w_l = w_in[l].astype(BF16)
        w_qkv, w_br = w_l[:, :d_qkv], w_l[:, d_qkv:]
        pa, pg, pc, wo = (p_attn[l].astype(BF16), p_gmlp[l].astype(BF16),
                          p_conv[l].astype(BF16), w_o[l].astype(BF16))
        ws_p = ws_tril[l].astype(BF16)
        bs_p = jnp.broadcast_to(gmlp_bs[l][:, :, None], (GMLP_GROUPS, CHUNK, LANES))
        ws_s = jnp.einsum("ab,gij->gaibj", eye_s, ws_tril[l][:, :seq_s, :seq_s]).reshape(
            GMLP_GROUPS, CHUNK, CHUNK).astype(BF16)
        bs_s = jnp.broadcast_to(jnp.tile(gmlp_bs[l][:, :seq_s], (1, CHUNK // seq_s))[:, :, None],
                                (GMLP_GROUPS, CHUNK, LANES))

        def mixer(x, tm):
            if l % 2 == 0:
                i = l // 2
                return _dense_ffn(x, ffn_gate[i].astype(BF16), ffn_up[i].astype(BF16),
                                  ffn_down[i].astype(BF16), row2(ln2_g[l]), row2(ln2_b[l]),
                                  tm=tm, alpha=alpha)
            i = l // 2
            return _moe_ffn(x, router_pad[i], moe_gate[i], moe_up[i], moe_down[i],
                            row2(ln2_g[l]), row2(ln2_b[l]),
                            tm=_token_tile(x.shape[0], MOE_TILE), alpha=alpha, n_experts=n_experts)

        q0, q1, kb, vt, *kv_p = _qkv_proj(yp, w_qkv, kv_p, layer=l, depth=depth, prompt=True, tm=tm_p,
                                          seq_len=seq_p, tk=tq, q_scale=q_scale)
        o = _prompt_attention(lam, sigma, q0, q1, kb, vt, bias_off, bias_diag,
                              subln_g[l].reshape(V_DIM, 1), seq_len=seq_p, out_scale=out_scale)
        part, ga, ztail = _branch_proj(yp, w_br, row2(gmlp_ln_g[l]), row2(gmlp_ln_b[l]), ws_p, bs_p,
                                       conv_w[l], pg, pc, None, prompt=True, tm=tm_p,
                                       seq_len=seq_p, period=1)
        x1 = _mix_proj(yp, o, ga, part, pa, wo, row2(ln1_g[l]), row2(ln1_b[l]), tm=tm_p, alpha=alpha)
        yp = mixer(x1, tm_p)
        tiles_per_seq = seq_p // tm_p
        outs["cp"].append(ztail.reshape(nb_p, tiles_per_seq, SUBLANES, cd)[:, -1, SUBLANES - (CONV_WIDTH - 1):])

        qs, *kv_s = _qkv_proj(ys, w_qkv, kv_s, layer=l, depth=depth, prompt=False, tm=tm_s,
                              seq_len=seq_s, tk=tq, q_scale=q_scale)
        q6 = qs.reshape(nb_s, seq_s, N_KV_HEADS, REP, 2, HEAD_DIM)
        eye_m = jnp.eye(2, dtype=F32)
        qbd = jnp.einsum("btgrmd,mn->bmdngrt", q6, eye_m).reshape(nb_s, QK_DIM, n_cols)
        qbd = jnp.pad(qbd, ((0, 0), (0, 0), (0, pad_cols))).astype(BF16)
        pad_new = ((0, 0), (0, NEW_ROWS - seq_s * N_KV_HEADS), (0, 0))
        knew = jnp.pad(kv_s[0][l].reshape(nb_s, seq_s * N_KV_HEADS, QK_DIM), pad_new)
        vnew = jnp.pad(kv_s[1][l].reshape(nb_s, seq_s * N_KV_HEADS, V_DIM), pad_new)
        o_s = _sample_attention(page_table, lam, qbd, cache_k2, cache_v2, l, base_bias, pagevec,
                                knew, vnew, bias_new, subln_g[l].reshape(1, V_DIM), seq_s=seq_s,
                                out_scale=out_scale)
        o_s = o_s.reshape(nb_s, N_KV_HEADS, REP, seq_s, V_DIM).transpose(0, 3, 1, 2, 4).reshape(n_s, N_HEADS * V_DIM)
        prev = state_conv[l]
        zeros = jnp.zeros((nb_s, seq_s - 1, cd), F32)
        fix1 = jnp.concatenate([prev[:, 1:2], zeros], axis=1).reshape(n_s, cd)
        fix2 = jnp.concatenate([prev, zeros[:, 1:]], axis=1).reshape(n_s, cd)
        part_s, ga_s, z_s, zv_s = _branch_proj(ys, w_br, row2(gmlp_ln_g[l]), row2(gmlp_ln_b[l]), ws_s,
                                               bs_s, conv_w[l], pg, pc, (fix1, fix2), prompt=False,
                                               tm=tm_s, seq_len=seq_s, period=seq_s)
        x1s = _mix_proj(ys, o_s, ga_s, part_s, pa, wo, row2(ln1_g[l]), row2(ln1_b[l]), tm=tm_s, alpha=alpha)
        ys = mixer(x1s, tm_s)
        outs["cs"].append(z_s.reshape(nb_s, seq_s, cd)[:, seq_s - (CONV_WIDTH - 1):])
        outs["gs"].append(zv_s.reshape(nb_s, seq_s, -1))

    return (yp.reshape(nb_p, seq_p, d_model), ys.reshape(nb_s, seq_s, d_model),
            kv_p[0].reshape(depth, nb_p, seq_p, N_KV_HEADS, QK_DIM),
            kv_p[1].reshape(depth, nb_p, seq_p, N_KV_HEADS, V_DIM), jnp.stack(outs["cp"]),
            kv_s[0].reshape(depth, nb_s, seq_s, N_KV_HEADS, QK_DIM),
            kv_s[1].reshape(depth, nb_s, seq_s, N_KV_HEADS, V_DIM),
            jnp.stack(outs["cs"]), jnp.stack(outs["gs"]))
```

```python
import functools
import math

import jax
import jax.numpy as jnp
import numpy as np
from jax import lax
from jax.experimental import pallas as pl
from jax.experimental.pallas import tpu as pltpu

F32 = jnp.float32
BF16 = jnp.bfloat16

N_HEADS = 8
N_KV_HEADS = 4
REP = N_HEADS // N_KV_HEADS
HEAD_DIM = 64
QK_DIM = 2 * HEAD_DIM
V_DIM = 2 * HEAD_DIM
GMLP_GROUPS = 4
CHUNK = 128
CONV_WIDTH = 3
N_BRANCH = 3
LN_EPS = 1e-5
LOG2E = 1.4426950408889634
NEG_BIG = -1e30

LANES = 128
SUBLANES = 8
VMEM_LIMIT = 56 * 1024 * 1024

ATTN_BLOCK = 256
ATTN_UNROLL = 4
PAGE_LEAD = 2
PAGE_SLOTS = PAGE_LEAD + 1
PAGES_PER_STEP = 16
MOE_TILE = 1024
ONES_ROWS = 16
NEW_ROWS = 16


def _cparams(sem):
    return pltpu.CompilerParams(dimension_semantics=sem, vmem_limit_bytes=VMEM_LIMIT)


def _resident(shape):
    nd = len(shape)
    return pl.BlockSpec(shape, lambda *_: (0,) * nd, pipeline_mode=pl.Buffered(1))


def _layer_norm(v, g, b):
    mu = jnp.mean(v, axis=-1, keepdims=True)
    d = v - mu
    var = jnp.mean(d * d, axis=-1, keepdims=True)
    return d * lax.rsqrt(var + LN_EPS) * g + b


def _dot(a, b):
    return jnp.dot(a, b, preferred_element_type=F32)


def _qkv_kernel(x_ref, w_ref, *refs, prompt, tk, q_scale, n_alias):
    out_refs = refs[n_alias:]
    d_q = N_HEADS * QK_DIM
    d_k = N_KV_HEADS * QK_DIM
    tm = x_ref.shape[0]
    xb = x_ref[...].astype(BF16)
    q = _dot(xb, w_ref[:, 0:d_q]) * q_scale
    k = _dot(xb, w_ref[:, d_q:d_q + d_k])
    v = _dot(xb, w_ref[:, d_q + d_k:])
    kf_ref, vf_ref = out_refs[-2:]
    for g in range(N_KV_HEADS):
        rows = pl.ds(g, tm, stride=N_KV_HEADS)
        kf_ref[rows, :] = k[:, g * QK_DIM:(g + 1) * QK_DIM]
        vf_ref[rows, :] = v[:, g * V_DIM:(g + 1) * V_DIM]
    if prompt:
        q0_ref, q1_ref, kb_ref, vt_ref = out_refs[:4]
        first_half = (lax.broadcasted_iota(jnp.int32, q.shape, 1) % QK_DIM) < HEAD_DIM
        q0_ref[...] = jnp.where(first_half, q, 0.0).astype(BF16)
        q1_ref[...] = jnp.where(first_half, 0.0, q).astype(BF16)
        kb_ref[...] = k.astype(BF16)
        for g in range(N_KV_HEADS):
            for j in range(tm // tk):
                blk = v[j * tk:(j + 1) * tk, g * V_DIM:(g + 1) * V_DIM]
                ones = jnp.ones((ONES_ROWS, tk), F32)
                vt_ref[g, j] = jnp.concatenate([blk.T, ones], axis=0).astype(BF16)
    else:
        out_refs[0][...] = q


def _qkv_proj(x, w_qkv, kv_stacks, *, layer, depth, prompt, tm, seq_len, tk, q_scale):
    n, d = x.shape
    d_q = N_HEADS * QK_DIM
    d_kv = N_KV_HEADS * QK_DIM
    nt = n // tm
    row = lambda i: (i, 0)
    in_specs = [pl.BlockSpec((tm, d), row), _resident(w_qkv.shape)]
    if prompt:
        tps = seq_len // tm
        nb = n // seq_len
        out_shape = [
            jax.ShapeDtypeStruct((n, d_q), BF16), jax.ShapeDtypeStruct((n, d_q), BF16),
            jax.ShapeDtypeStruct((n, d_kv), BF16),
            jax.ShapeDtypeStruct((nb, N_KV_HEADS, seq_len // tk, V_DIM + ONES_ROWS, tk), BF16),
        ]
        out_specs = [
            pl.BlockSpec((tm, d_q), row), pl.BlockSpec((tm, d_q), row),
            pl.BlockSpec((tm, d_kv), row),
            pl.BlockSpec((None, N_KV_HEADS, tm // tk, V_DIM + ONES_ROWS, tk),
                         lambda i: (i // tps, 0, i % tps, 0, 0)),
        ]
    else:
        out_shape = [jax.ShapeDtypeStruct((n, d_q), F32)]
        out_specs = [pl.BlockSpec((tm, d_q), row)]
    stack_shape = jax.ShapeDtypeStruct((depth, n * N_KV_HEADS, QK_DIM), F32)
    stack_spec = pl.BlockSpec((None, tm * N_KV_HEADS, QK_DIM), lambda i: (layer, i, 0))
    n_out = len(out_shape)
    out_shape += [stack_shape, stack_shape]
    out_specs += [stack_spec, stack_spec]
    in_specs += [pl.BlockSpec(memory_space=pl.ANY)] * 2
    args = [x, w_qkv, *kv_stacks]
    aliases = {2: n_out, 3: n_out + 1}
    return pl.pallas_call(
        functools.partial(_qkv_kernel, prompt=prompt, tk=tk, q_scale=q_scale, n_alias=len(aliases)),
        grid=(nt,), in_specs=in_specs, out_specs=out_specs, out_shape=out_shape,
        input_output_aliases=aliases,
        compiler_params=_cparams(("arbitrary",)), name="qkv_proj",
    )(*args)


def _prompt_attn_kernel(lam_ref, sig_ref, q0_ref, q1_ref, k_ref, vt_ref, boff_ref, bdiag_ref,
                        gain_ref, o_ref, acc_ref, s_ref, *, tq, tk, out_scale):
    g = pl.program_id(1)
    qi = pl.program_id(2)
    n_state = REP * 2

    def scores(j, h):
        r, m = divmod(h, 2)
        kj = k_ref[pl.ds(pl.multiple_of(j * tk, tk), tk), :]
        qh = (q0_ref if m == 0 else q1_ref)[:, r * QK_DIM:(r + 1) * QK_DIM]
        return lax.dot_general(kj, qh, (((1,), (1,)), ((), ())), preferred_element_type=F32)

    def block(j, bias_ref, carry, prefetch):
        m_all = carry
        vtj = vt_ref[j]
        dist_blocks = (qi - j).astype(F32) * float(tq)
        m_out = []
        for h in range(n_state):
            r = h // 2
            c_j = -sig_ref[g * REP + r] * dist_blocks
            s_t = s_ref[h] + bias_ref[r]
            if prefetch:
                s_ref[h] = scores(j + 1, h)
            m_new = jnp.maximum(m_all[h], jnp.max(s_t, axis=0, keepdims=True) + c_j)
            alpha = jnp.exp2(m_all[h] - m_new)
            p_t = jnp.exp2(s_t - (m_new - c_j))
            m_out.append(m_new)
            acc_ref[h] = acc_ref[h] * alpha + _dot(vtj, p_t.astype(BF16))
        return tuple(m_out)

    acc_ref[...] = jnp.zeros_like(acc_ref)
    for h in range(n_state):
        s_ref[h] = scores(0, h)
    init = tuple(jnp.full((1, tq), NEG_BIG, F32) for _ in range(n_state))
    def group(jj, c):
        for u in range(ATTN_UNROLL):
            c = block(ATTN_UNROLL * jj + u, boff_ref, c, True)
        return c

    n_grouped = (qi // ATTN_UNROLL) * ATTN_UNROLL
    carry = lax.fori_loop(0, qi // ATTN_UNROLL, group, init)
    carry = lax.fori_loop(n_grouped, qi, lambda j, c: block(j, boff_ref, c, True), carry)
    block(qi, bdiag_ref, carry, False)

    lam = lam_ref[0]
    gain = gain_ref[...] * out_scale
    for r in range(REP):
        a0, a1 = acc_ref[2 * r], acc_ref[2 * r + 1]
        o_t = a0[:V_DIM] / a0[V_DIM:V_DIM + 1] - lam * (a1[:V_DIM] / a1[V_DIM:V_DIM + 1])
        ms = jnp.mean(o_t * o_t, axis=0, keepdims=True)
        o_t = o_t * lax.rsqrt(ms + LN_EPS) * gain
        o_ref[:, r * V_DIM:(r + 1) * V_DIM] = o_t.T.astype(o_ref.dtype)


def _prompt_attention(lam, sigma, q0, q1, kb, vt, bias_off, bias_diag, gain, *, seq_len, out_scale):
    n = q0.shape[0]
    nb = n // seq_len
    tq = tk = ATTN_BLOCK
    nq = seq_len // tq
    smem = pl.BlockSpec(memory_space=pltpu.SMEM)
    qspec = pl.BlockSpec((tq, REP * QK_DIM), lambda b, g, i: (b * nq + i, g))
    return pl.pallas_call(
        functools.partial(_prompt_attn_kernel, tq=tq, tk=tk, out_scale=out_scale),
        grid=(nb, N_KV_HEADS, nq),
        in_specs=[
            smem, smem, qspec, qspec,
            pl.BlockSpec((seq_len, QK_DIM), lambda b, g, i: (b, g)),
            pl.BlockSpec((None, None, seq_len // tk, V_DIM + ONES_ROWS, tk),
                         lambda b, g, i: (b, g, 0, 0, 0)),
            pl.BlockSpec((None, REP, tk, tq), lambda b, g, i: (g, 0, 0, 0)),
            pl.BlockSpec((None, REP, tk, tq), lambda b, g, i: (g, 0, 0, 0)),
            pl.BlockSpec((V_DIM, 1), lambda b, g, i: (0, 0)),
        ],
        out_specs=pl.BlockSpec((tq, REP * V_DIM), lambda b, g, i: (b * nq + i, g)),
        out_shape=jax.ShapeDtypeStruct((n, N_HEADS * V_DIM), BF16),
        scratch_shapes=[pltpu.VMEM((REP * 2, V_DIM + ONES_ROWS, tq), F32),
                        pltpu.VMEM((REP * 2, tk, tq), F32)],
        compiler_params=_cparams(("arbitrary", "arbitrary", "arbitrary")), name="prompt_attn",
    )(lam, sigma, q0, q1, kb, vt, bias_off, bias_diag, gain)


def _sample_attn_kernel(pt_ref, lam_ref, qbd_ref, k_hbm, v_hbm, base_ref, pagevec_ref, knew_ref,
                        vnew_ref, bnew_ref, gain_ref, o_ref, m_ref, l_ref, acc_ref, sa_ref, sb_ref,
                        kbuf, vbuf, sem, *, n_pages_step, n_chunks, layer, out_scale):
    P = n_pages_step
    b = pl.program_id(0)
    c = pl.program_id(1)
    n_c = n_chunks + 1
    step_id = b * n_c + c
    n_steps = pl.num_programs(0) * n_c
    rows_g = o_ref.shape[1]
    half = N_KV_HEADS * rows_g

    def k_copy(bb, cc, slot, p):
        return pltpu.make_async_copy(k_hbm.at[layer, pt_ref[bb, cc * P + p]], kbuf.at[slot, p],
                                     sem.at[0, slot])

    def v_copy(bb, cc, slot, p):
        return pltpu.make_async_copy(v_hbm.at[layer, pt_ref[bb, cc * P + p]], vbuf.at[slot, p],
                                     sem.at[1, slot])

    def request(s):
        bb = s // n_c
        cc = s % n_c
        slot = s % PAGE_SLOTS

        @pl.when((s < n_steps) & (cc < n_chunks))
        def _():
            for p in range(P):
                k_copy(bb, cc, slot, p).start()

        @pl.when((s < n_steps) & (cc >= 1))
        def _():
            for p in range(P):
                v_copy(bb, cc - 1, slot, p).start()

    @pl.when(step_id == 0)
    def _():
        for s in range(PAGE_LEAD):
            request(jnp.int32(s))

    request(step_id + PAGE_LEAD)
    slot = step_id % PAGE_SLOTS

    @pl.when(c < n_chunks)
    def _():
        for p in range(P):
            k_copy(b, c, slot, p).wait()

    @pl.when(c >= 1)
    def _():
        for p in range(P):
            v_copy(b, c - 1, slot, p).wait()

    @pl.when(c == 0)
    def _():
        m_ref[...] = jnp.full_like(m_ref, NEG_BIG)
        l_ref[...] = jnp.zeros_like(l_ref)
        acc_ref[...] = jnp.zeros_like(acc_ref)

    qbd = qbd_ref[...]

    def col_bcast(row):
        return jnp.transpose(jnp.broadcast_to(row, (LANES, LANES)))

    def update(s_tiles, v_tiles):
        m_old = m_ref[...]
        mx = s_tiles[0].max(axis=0, keepdims=True)
        for s_t in s_tiles[1:]:
            mx = jnp.maximum(mx, s_t.max(axis=0, keepdims=True))
        m_new = jnp.maximum(m_old, mx)
        alpha = jnp.exp2(m_old - m_new)
        lsum = jnp.zeros_like(m_old)
        pv = None
        for s_t, v_t in zip(s_tiles, v_tiles):
            p_t = jnp.exp2(s_t - m_new)
            lsum = lsum + jnp.sum(p_t, axis=0, keepdims=True)
            part = lax.dot_general(p_t.astype(BF16), v_t, (((0,), (0,)), ((), ())),
                                   preferred_element_type=F32)
            pv = part if pv is None else pv + part
        m_ref[...] = m_new
        l_ref[...] = alpha * l_ref[...] + lsum
        acc_ref[...] = acc_ref[...] * col_bcast(alpha) + pv

    def step(s_write, s_read):
        if s_write is not None:
            base = base_ref[...]
            page0 = (c * P).astype(F32)
            for p in range(P):
                s_write[p] = (_dot(kbuf[slot, p].astype(BF16), qbd) + base
                              + pagevec_ref[...] * (page0 + float(p)))
        if s_read is not None:
            update([s_read[p] for p in range(P)], [vbuf[slot, p].astype(BF16) for p in range(P)])

    last_read = sb_ref if n_chunks % 2 == 0 else sa_ref

    @pl.when(c == 0)
    def _():
        step(sa_ref, None)

    @pl.when((c % 2 == 1) & (c < n_chunks))
    def _():
        step(sb_ref, sa_ref)

    @pl.when((c % 2 == 0) & (c > 0) & (c < n_chunks))
    def _():
        step(sa_ref, sb_ref)

    @pl.when(c == n_chunks)
    def _():
        step(None, last_read)
        s_new = _dot(knew_ref[...].astype(BF16), qbd) + bnew_ref[...]
        update([s_new], [vnew_ref[...].astype(BF16)])
        l_col = col_bcast(l_ref[...])
        acc = acc_ref[...]
        o = acc[:half] / l_col[:half] - lam_ref[0] * (acc[half:2 * half] / l_col[half:2 * half])
        ms = jnp.mean(o * o, axis=-1, keepdims=True)
        o = o * lax.rsqrt(ms + LN_EPS) * (gain_ref[...] * out_scale)
        for g in range(N_KV_HEADS):
            o_ref[g] = o[g * rows_g:(g + 1) * rows_g].astype(o_ref.dtype)


def _sample_attention(page_table, lam, qbd, cache_k, cache_v, layer, base, pagevec, knew, vnew,
                      bias_new, gain, *, seq_s, out_scale):
    nb, n_pages = page_table.shape
    P = PAGES_PER_STEP
    page_shape = cache_k.shape[2:]
    rows_g = REP * seq_s
    smem = pl.BlockSpec(memory_space=pltpu.SMEM)

    n_chunks = n_pages // P
    hbm = pl.BlockSpec(memory_space=pl.ANY)
    ring = (PAGE_SLOTS, P, page_shape[0], LANES)
    const2 = lambda b, c, pt: (0, 0)
    per_seq = lambda b, c, pt: (b, 0, 0)
    grid_spec = pltpu.PrefetchScalarGridSpec(
        num_scalar_prefetch=1,
        grid=(nb, n_chunks + 1),
        in_specs=[smem, pl.BlockSpec((None,) + qbd.shape[1:], per_seq), hbm, hbm]
        + [pl.BlockSpec(base.shape, const2), pl.BlockSpec(pagevec.shape, const2),
           pl.BlockSpec((None,) + knew.shape[1:], per_seq),
           pl.BlockSpec((None,) + vnew.shape[1:], per_seq),
           pl.BlockSpec(bias_new.shape, const2), pl.BlockSpec(gain.shape, const2)],
        out_specs=pl.BlockSpec((None, N_KV_HEADS, rows_g, V_DIM), lambda b, c, pt: (b, 0, 0, 0)),
        scratch_shapes=[pltpu.VMEM((1, LANES), F32), pltpu.VMEM((1, LANES), F32),
                        pltpu.VMEM((LANES, V_DIM), F32),
                        pltpu.VMEM((P, page_shape[0], LANES), F32),
                        pltpu.VMEM((P, page_shape[0], LANES), F32),
                        pltpu.VMEM(ring, F32), pltpu.VMEM(ring, F32),
                        pltpu.SemaphoreType.DMA((2, PAGE_SLOTS))],
    )
    return pl.pallas_call(
        functools.partial(_sample_attn_kernel, n_pages_step=P, n_chunks=n_chunks, layer=layer,
                          out_scale=out_scale),
        grid_spec=grid_spec,
        out_shape=jax.ShapeDtypeStruct((nb, N_KV_HEADS, rows_g, V_DIM), BF16),
        compiler_params=_cparams(("arbitrary", "arbitrary")), name="sample_attn",
    )(page_table, lam, qbd, cache_k, cache_v, base, pagevec, knew, vnew, bias_new, gain)


def _branch_kernel(x_ref, w_ref, lng_ref, lnb_ref, ws_ref, bs_ref, cw_ref, pg_ref, pc_ref, *refs,
                   prompt, tiles_per_seq, period):
    d_model = x_ref.shape[1]
    gd = pg_ref.shape[0]
    cd = pc_ref.shape[0]
    tm = x_ref.shape[0]
    if prompt:
        part_ref, ga_ref, ztail_ref, carry_ref = refs
    else:
        fix1_ref, fix2_ref, part_ref, ga_ref, z_ref, zv_ref = refs
    xb = x_ref[...].astype(BF16)

    def proj(lo, width):
        return _dot(xb, w_ref[:, lo:lo + width])

    z_u = jax.nn.gelu(proj(0, gd))
    z_v = _layer_norm(jax.nn.gelu(proj(gd, gd)), lng_ref[...], lnb_ref[...])
    if not prompt:
        zv_ref[...] = z_v
    gw = gd // GMLP_GROUPS
    z_vb = z_v.astype(BF16)
    rows = []
    for n in range(tm // CHUNK):
        cols = []
        for g in range(GMLP_GROUPS):
            blk = z_vb[n * CHUNK:(n + 1) * CHUNK, g * gw:(g + 1) * gw]
            cols.append(_dot(ws_ref[g], blk) + bs_ref[g])
        rows.append(jnp.concatenate(cols, axis=1))
    s = jnp.concatenate(rows, axis=0)
    y_g = _dot((z_u * s).astype(BF16), pg_ref[...])

    off = 2 * gd
    c_b = proj(off, cd)
    z = proj(off + cd, cd) * proj(off + 2 * cd, cd)
    zr1 = pltpu.roll(z, 1, 0)
    zr2 = pltpu.roll(z, 2, 0)
    if prompt:
        @pl.when(pl.program_id(0) % tiles_per_seq == 0)
        def _():
            carry_ref[...] = jnp.zeros_like(carry_ref)

        prev = carry_ref[...]
        r8 = lax.broadcasted_iota(jnp.int32, (SUBLANES, cd), 0)
        top1 = jnp.where(r8 < 1, pltpu.roll(prev, 1, 0), zr1[:SUBLANES])
        top2 = jnp.where(r8 < 2, pltpu.roll(prev, 2, 0), zr2[:SUBLANES])
        zm1 = jnp.concatenate([top1, zr1[SUBLANES:]], axis=0)
        zm2 = jnp.concatenate([top2, zr2[SUBLANES:]], axis=0)
        tail = z[tm - SUBLANES:]
        carry_ref[...] = tail
        ztail_ref[...] = tail
    else:
        pos = lax.broadcasted_iota(jnp.int32, z.shape, 0) % period
        zm1 = jnp.where(pos >= 1, zr1, fix1_ref[...])
        zm2 = jnp.where(pos >= 2, zr2, fix2_ref[...])
        z_ref[...] = z
    cw = cw_ref[...]
    y = cw[0:1] * zm2 + cw[1:2] * zm1 + cw[2:3] * z
    y_c = _dot((c_b * y).astype(BF16), pc_ref[...])

    off = 2 * gd + 3 * cd
    ga_ref[...] = jax.nn.sigmoid(proj(off, d_model)).astype(ga_ref.dtype)
    part_ref[...] = (jax.nn.sigmoid(proj(off + d_model, d_model)) * y_g
                     + jax.nn.sigmoid(proj(off + 2 * d_model, d_model)) * y_c)


def _branch_proj(x, w_br, ln_g, ln_b, ws_mat, bs_mat, conv_w, p_gmlp, p_conv, fixes, *,
                 prompt, tm, seq_len, period):
    n, d = x.shape
    gd, cd = p_gmlp.shape[0], p_conv.shape[0]
    nt = n // tm
    row = lambda i: (i, 0)
    in_specs = [pl.BlockSpec((tm, d), row), _resident(w_br.shape), _resident(ln_g.shape),
                _resident(ln_b.shape), _resident(ws_mat.shape), _resident(bs_mat.shape),
                _resident(conv_w.shape), _resident(p_gmlp.shape), _resident(p_conv.shape)]
    args = [x, w_br, ln_g, ln_b, ws_mat, bs_mat, conv_w, p_gmlp, p_conv]
    out_shape = [jax.ShapeDtypeStruct((n, d), F32), jax.ShapeDtypeStruct((n, d), BF16)]
    out_specs = [pl.BlockSpec((tm, d), row), pl.BlockSpec((tm, d), row)]
    scratch = []
    if prompt:
        out_shape.append(jax.ShapeDtypeStruct((nt * SUBLANES, cd), F32))
        out_specs.append(pl.BlockSpec((SUBLANES, cd), row))
        scratch.append(pltpu.VMEM((SUBLANES, cd), F32))
    else:
        in_specs += [pl.BlockSpec((tm, cd), row), pl.BlockSpec((tm, cd), row)]
        args += list(fixes)
        out_shape += [jax.ShapeDtypeStruct((n, cd), F32), jax.ShapeDtypeStruct((n, gd), F32)]
        out_specs += [pl.BlockSpec((tm, cd), row), pl.BlockSpec((tm, gd), row)]
    return pl.pallas_call(
        functools.partial(_branch_kernel, prompt=prompt,
                          tiles_per_seq=max(seq_len // tm, 1), period=period),
        grid=(nt,), in_specs=in_specs, out_specs=out_specs, out_shape=out_shape,
        scratch_shapes=scratch,
        compiler_params=_cparams(("arbitrary",)), name="branch_proj",
    )(*args)


def _mix_kernel(x_ref, o_ref, ga_ref, part_ref, pa_ref, wo_ref, g_ref, b_ref, out_ref, *, alpha):
    y_a = _dot(o_ref[...], pa_ref[...])
    mix = ga_ref[...].astype(F32) * y_a + part_ref[...]
    h = _dot(mix.astype(BF16), wo_ref[...])
    out_ref[...] = _layer_norm(alpha * x_ref[...] + h, g_ref[...], b_ref[...])


def _mix_proj(x, o, ga, part, p_attn, w_o, ln_g, ln_b, *, tm, alpha):
    n, d = x.shape
    row = lambda i: (i, 0)
    return pl.pallas_call(
        functools.partial(_mix_kernel, alpha=alpha),
        grid=(n // tm,),
        in_specs=[pl.BlockSpec((tm, d), row), pl.BlockSpec((tm, o.shape[1]), row),
                  pl.BlockSpec((tm, d), row), pl.BlockSpec((tm, d), row),
                  _resident(p_attn.shape), _resident(w_o.shape),
                  _resident(ln_g.shape), _resident(ln_b.shape)],
        out_specs=pl.BlockSpec((tm, d), row),
        out_shape=jax.ShapeDtypeStruct((n, d), F32),
        compiler_params=_cparams(("arbitrary",)), name="mix_proj",
    )(x, o, ga, part, p_attn, w_o, ln_g, ln_b)


def _ffn_kernel(x_ref, wg_ref, wu_ref, wd_ref, g_ref, b_ref, out_ref, *, alpha, ff_chunk):
    x = x_ref[...]
    xb = x.astype(BF16)
    f = None
    for c0 in range(0, wg_ref.shape[1], ff_chunk):
        h = jax.nn.silu(_dot(xb, wg_ref[:, c0:c0 + ff_chunk])) * _dot(xb, wu_ref[:, c0:c0 + ff_chunk])
        part = _dot(h.astype(BF16), wd_ref[c0:c0 + ff_chunk, :])
        f = part if f is None else f + part
    out_ref[...] = _layer_norm(alpha * x + f, g_ref[...], b_ref[...])


def _ff_chunk(d_ff):
    for c in (1408, 1024, 512, 256, 128):
        if d_ff % c == 0:
            return c
    return d_ff


def _dense_ffn(x, wg, wu, wd, ln_g, ln_b, *, tm, alpha):
    n, d = x.shape
    row = lambda i: (i, 0)
    return pl.pallas_call(
        functools.partial(_ffn_kernel, alpha=alpha, ff_chunk=_ff_chunk(wg.shape[1])),
        grid=(n // tm,),
        in_specs=[pl.BlockSpec((tm, d), row), _resident(wg.shape), _resident(wu.shape),
                  _resident(wd.shape), _resident(ln_g.shape), _resident(ln_b.shape)],
        out_specs=pl.BlockSpec((tm, d), row),
        out_shape=jax.ShapeDtypeStruct((n, d), F32),
        compiler_params=_cparams(("arbitrary",)), name="dense_ffn",
    )(x, wg, wu, wd, ln_g, ln_b)


def _router_kernel(x_ref, r_ref, tri_ref, comb_ref, rank_ref, cnt_ref, *, n_experts):
    logits = jnp.dot(x_ref[...], r_ref[...], preferred_element_type=F32,
                     precision=lax.Precision.HIGHEST)
    lt = logits.T[:n_experts]
    row = lax.broadcasted_iota(jnp.int32, lt.shape, 0)
    m1 = jnp.max(lt, axis=0, keepdims=True)
    i1 = jnp.min(jnp.where(lt == m1, row, n_experts), axis=0, keepdims=True)
    rest = jnp.where(row == i1, -jnp.inf, lt)
    m2 = jnp.max(rest, axis=0, keepdims=True)
    i2 = jnp.min(jnp.where(rest == m2, row, n_experts), axis=0, keepdims=True)
    t = jnp.exp(m2 - m1)
    w1 = 1.0 / (1.0 + t)
    comb = jnp.where(row == i1, w1, 0.0) + jnp.where(row == i2, t * w1, 0.0)
    routed = jnp.where(comb > 0.0, 1.0, 0.0)
    lhs = jnp.concatenate([routed, jnp.zeros_like(routed)], axis=0).astype(BF16)
    comb_ref[...] = comb
    rank_ref[...] = _dot(lhs, tri_ref[...])[:n_experts]
    cnt_ref[...] = jnp.broadcast_to(jnp.sum(routed, axis=1, keepdims=True), cnt_ref.shape)


def _moe_kernel(cnt_ref, x_ref, comb_ref, rank_ref, wg_ref, wu_ref, wd_ref, g_ref, b_ref, out_ref,
                xb_ref, xs_ref, y_ref, *, alpha, rows):
    i = pl.program_id(0)
    e = pl.program_id(1)
    c = pl.program_id(2)
    last_c = pl.num_programs(2) - 1
    t = x_ref.shape[0]
    n_blk = (cnt_ref[i, e] + rows - 1) // rows
    comb_row = comb_ref[pl.ds(e, 1), :]
    rank_row = rank_ref[pl.ds(e, 1), :]

    @pl.when((e == 0) & (c == 0))
    def _():
        xb_ref[...] = x_ref[...].astype(BF16)
        out_ref[...] = jnp.zeros_like(out_ref)

    def block_rows(b):
        return pl.ds(pl.multiple_of(b * rows, rows), rows)

    def selection(b, weighted):
        slot = (lax.broadcasted_iota(jnp.int32, (rows, t), 0) + b * rows).astype(F32)
        hit = (rank_row == slot) & (comb_row > 0.0)
        return jnp.where(hit, comb_row if weighted else 1.0, 0.0).astype(BF16)

    def expert(b):
        xs = xs_ref[block_rows(b), :]
        h = jax.nn.silu(_dot(xs, wg_ref[...])) * _dot(xs, wu_ref[...])
        return _dot(h.astype(BF16), wd_ref[...])

    def for_blocks(body):
        def step(b, carry):
            body(b)
            return carry
        lax.fori_loop(0, n_blk, step, 0)

    @pl.when(c == 0)
    def _():
        def gather_and_run(b):
            xs_ref[block_rows(b), :] = _dot(selection(b, False), xb_ref[...]).astype(BF16)
            y_ref[block_rows(b), :] = expert(b)
        for_blocks(gather_and_run)

    @pl.when(c > 0)
    def _():
        def run(b):
            y_ref[block_rows(b), :] += expert(b)
        for_blocks(run)

    @pl.when(c == last_c)
    def _():
        def scatter(b):
            out_ref[...] += lax.dot_general(selection(b, True), y_ref[block_rows(b), :].astype(BF16),
                                            (((0,), (0,)), ((), ())), preferred_element_type=F32)
        for_blocks(scatter)

    @pl.when((e == pl.num_programs(1) - 1) & (c == last_c))
    def _():
        out_ref[...] = _layer_norm(alpha * x_ref[...] + out_ref[...], g_ref[...], b_ref[...])


def _moe_block_rows(tm, n_experts, top_k=2):
    mean = tm * top_k // n_experts
    return min(tm, -(-(mean + 32) // 32) * 32)


def _moe_ffn(x, router_pad, wg, wu, wd, ln_g, ln_b, *, tm, alpha, n_experts):
    n, d = x.shape
    nt = n // tm
    d_ff = wg.shape[2]
    fc = _ff_chunk(d_ff)
    wg, wu, wd = wg.astype(BF16), wu.astype(BF16), wd.astype(BF16)
    rows = _moe_block_rows(tm, n_experts)
    cap = -(-tm // rows) * rows
    tri = jnp.asarray(np.arange(tm)[:, None] < np.arange(tm)[None, :], dtype=BF16)
    comb, rank, cnt = pl.pallas_call(
        functools.partial(_router_kernel, n_experts=n_experts),
        grid=(nt,),
        in_specs=[pl.BlockSpec((tm, d), lambda i: (i, 0)), _resident(router_pad.shape),
                  _resident(tri.shape)],
        out_specs=[pl.BlockSpec((n_experts, tm), lambda i: (0, i)),
                   pl.BlockSpec((n_experts, tm), lambda i: (0, i)),
                   pl.BlockSpec((None, n_experts, LANES), lambda i: (i, 0, 0))],
        out_shape=[jax.ShapeDtypeStruct((n_experts, n), F32), jax.ShapeDtypeStruct((n_experts, n), F32),
                   jax.ShapeDtypeStruct((nt, n_experts, LANES), F32)],
        compiler_params=_cparams(("arbitrary",)), name="moe_router",
    )(x, router_pad, tri)
    counts = cnt[:, :, 0].astype(jnp.int32)
    row = lambda i, e, c, cnt: (i, 0)
    col = lambda i, e, c, cnt: (0, i)
    const = lambda i, e, c, cnt: (0, 0)
    grid_spec = pltpu.PrefetchScalarGridSpec(
        num_scalar_prefetch=1,
        grid=(nt, n_experts, d_ff // fc),
        in_specs=[pl.BlockSpec((tm, d), row, pipeline_mode=pl.Buffered(1)),
                  pl.BlockSpec((n_experts, tm), col), pl.BlockSpec((n_experts, tm), col),
                  pl.BlockSpec((None, d, fc), lambda i, e, c, cnt: (e, 0, c)),
                  pl.BlockSpec((None, d, fc), lambda i, e, c, cnt: (e, 0, c)),
                  pl.BlockSpec((None, fc, d), lambda i, e, c, cnt: (e, c, 0)),
                  pl.BlockSpec(ln_g.shape, const), pl.BlockSpec(ln_b.shape, const)],
        out_specs=pl.BlockSpec((tm, d), row),
        scratch_shapes=[pltpu.VMEM((tm, d), BF16), pltpu.VMEM((cap, d), BF16),
                        pltpu.VMEM((cap, d), F32)],
    )
    return pl.pallas_call(
        functools.partial(_moe_kernel, alpha=alpha, rows=rows),
        grid_spec=grid_spec,
        out_shape=jax.ShapeDtypeStruct((n, d), F32),
        compiler_params=_cparams(("arbitrary", "arbitrary", "arbitrary")), name="moe_ffn",
    )(counts, x, comb, rank, wg, wu, wd, ln_g, ln_b)


def _token_tile(n, cap):
    t = cap
    while n % t:
        t //= 2
    return t


def kernel(x_prompt, x_sample, cache_k, cache_v, state_conv, page_table, w_in, lam_params, subln_g, gmlp_ln_g, gmlp_ln_b, gmlp_ws, gmlp_bs, conv_w, p_attn, p_gmlp, p_conv, w_o, ln1_g, ln1_b, ln2_g, ln2_b, ffn_gate, ffn_up, ffn_down, router, moe_gate, moe_up, moe_down):
    depth = w_in.shape[0]
    nb_p, seq_p, d_model = x_prompt.shape
    nb_s, seq_s, _ = x_sample.shape
    n_pool, page_size = cache_k.shape[1], cache_k.shape[2]
    n_pages = page_table.shape[1]
    past_len = n_pages * page_size
    n_experts = router.shape[2]
    alpha = (2 * depth) ** 0.25
    d_q = N_HEADS * QK_DIM
    d_kv = N_KV_HEADS * QK_DIM
    d_qkv = d_q + 2 * d_kv
    cd = p_conv.shape[1]
    q_scale = LOG2E * HEAD_DIM ** -0.5

    n_p = nb_p * seq_p
    n_s = nb_s * seq_s
    tm_p = _token_tile(seq_p, 512)
    tm_s = n_s
    tq = ATTN_BLOCK

    f32 = np.float32
    sigma_np = (np.exp2(-8.0 * np.arange(1, N_HEADS + 1) / N_HEADS) * LOG2E).astype(f32)
    sigma = jnp.asarray(sigma_np)
    rel = np.arange(tq, dtype=f32)[None, :] - np.arange(tq, dtype=f32)[:, None]
    bias_off_np = (-sigma_np.reshape(N_KV_HEADS, REP, 1, 1) * rel).astype(f32)
    bias_off = jnp.asarray(bias_off_np)
    bias_diag = jnp.asarray(np.where(rel >= 0, bias_off_np, f32(NEG_BIG)).astype(f32))

    n_cols = 2 * N_KV_HEADS * REP * seq_s
    pad_cols = LANES - n_cols
    col_shape = (2, N_KV_HEADS, REP, seq_s)
    pad1 = lambda a: np.pad(np.broadcast_to(a, col_shape).reshape(n_cols), (0, pad_cols))
    sig_cols = pad1(sigma_np.reshape(1, N_KV_HEADS, REP, 1))[None, :]
    t_cols = pad1(np.arange(seq_s, dtype=f32))[None, :]
    g_cols = pad1(np.arange(N_KV_HEADS).reshape(1, N_KV_HEADS, 1, 1))[None, :]
    row_id = np.arange(page_size * N_KV_HEADS)[:, None]
    key_row = (row_id // N_KV_HEADS).astype(f32)
    base_bias = jnp.asarray(np.where((row_id % N_KV_HEADS) == g_cols,
                                     -sig_cols * (f32(past_len) + t_cols - key_row), f32(NEG_BIG)).astype(f32))
    pagevec = jnp.asarray((sig_cols * f32(page_size)).astype(f32))
    new_id = np.arange(NEW_ROWS)[:, None]
    new_row = (new_id // N_KV_HEADS).astype(f32)
    new_ok = ((new_id % N_KV_HEADS) == g_cols) & (new_row <= t_cols) & (new_row < seq_s)
    bias_new = jnp.asarray(np.where(new_ok, -sig_cols * (t_cols - new_row), f32(NEG_BIG)).astype(f32))
    cache_k2 = cache_k.reshape(depth, n_pool, page_size * N_KV_HEADS, QK_DIM)
    cache_v2 = cache_v.reshape(depth, n_pool, page_size * N_KV_HEADS, V_DIM)

    ws_tril = jnp.tril(gmlp_ws)
    eye_s = jnp.eye(CHUNK // seq_s, dtype=F32)

    row2 = lambda a: a.reshape(1, -1)
    router_pad = jnp.pad(router, ((0, 0), (0, 0), (0, LANES - n_experts)))

    yp = x_prompt.reshape(n_p, d_model)
    ys = x_sample.reshape(n_s, d_model)
    outs = {k: [] for k in ("cp", "cs", "gs")}
    kv_p = [jnp.zeros((depth, n_p * N_KV_HEADS, QK_DIM), F32)] * 2
    kv_s = [jnp.zeros((depth, n_s * N_KV_HEADS, QK_DIM), F32)] * 2
    for l in range(depth):
        lam_init = 0.8 - 0.6 * math.exp(-0.3 * l)
        lp = lam_params[l].astype(F32)
        lam = (jnp.exp(jnp.sum(lp[0] * lp[1])) - jnp.exp(jnp.sum(lp[2] * lp[3])) + lam_init).reshape(1)
        out_scale = 1.0 - lam_init
        w_l = w_in[l].astype(BF16)
        w_qkv, w_br = w_l[:, :d_qkv], w_l[:, d_qkv:]
        pa, pg, pc, wo = (p_attn[l].astype(BF16), p_gmlp[l].astype(BF16),
                          p_conv[l].astype(BF16), w_o[l].astype(BF16))
        ws_p = ws_tril[l].astype(BF16)
        bs_p = jnp.broadcast_to(gmlp_bs[l][:, :, None], (GMLP_GROUPS, CHUNK, LANES))
        ws_s = jnp.einsum("ab,gij->gaibj", eye_s, ws_tril[l][:, :seq_s, :seq_s]).reshape(
            GMLP_GROUPS, CHUNK, CHUNK).astype(BF16)
        bs_s = jnp.broadcast_to(jnp.tile(gmlp_bs[l][:, :seq_s], (1, CHUNK // seq_s))[:, :, None],
                                (GMLP_GROUPS, CHUNK, LANES))

        def mixer(x, tm):
            if l % 2 == 0:
                i = l // 2
                return _dense_ffn(x, ffn_gate[i].astype(BF16), ffn_up[i].astype(BF16),
                                  ffn_down[i].astype(BF16), row2(ln2_g[l]), row2(ln2_b[l]),
                                  tm=tm, alpha=alpha)
            i = l // 2
            return _moe_ffn(x, router_pad[i], moe_gate[i], moe_up[i], moe_down[i],
                            row2(ln2_g[l]), row2(ln2_b[l]),
                            tm=_token_tile(x.shape[0], MOE_TILE), alpha=alpha, n_experts=n_experts)

        q0, q1, kb, vt, *kv_p = _qkv_proj(yp, w_qkv, kv_p, layer=l, depth=depth, prompt=True, tm=tm_p,
                                          seq_len=seq_p, tk=tq, q_scale=q_scale)
        o = _prompt_attention(lam, sigma, q0, q1, kb, vt, bias_off, bias_diag,
                              subln_g[l].reshape(V_DIM, 1), seq_len=seq_p, out_scale=out_scale)
        part, ga, ztail = _branch_proj(yp, w_br, row2(gmlp_ln_g[l]), row2(gmlp_ln_b[l]), ws_p, bs_p,
                                       conv_w[l], pg, pc, None, prompt=True, tm=tm_p,
                                       seq_len=seq_p, period=1)
        x1 = _mix_proj(yp, o, ga, part, pa, wo, row2(ln1_g[l]), row2(ln1_b[l]), tm=tm_p, alpha=alpha)
        yp = mixer(x1, tm_p)
        tiles_per_seq = seq_p // tm_p
        outs["cp"].append(ztail.reshape(nb_p, tiles_per_seq, SUBLANES, cd)[:, -1, SUBLANES - (CONV_WIDTH - 1):])

        qs, *kv_s = _qkv_proj(ys, w_qkv, kv_s, layer=l, depth=depth, prompt=False, tm=tm_s,
                              seq_len=seq_s, tk=tq, q_scale=q_scale)
        q6 = qs.reshape(nb_s, seq_s, N_KV_HEADS, REP, 2, HEAD_DIM)
        eye_m = jnp.eye(2, dtype=F32)
        qbd = jnp.einsum("btgrmd,mn->bmdngrt", q6, eye_m).reshape(nb_s, QK_DIM, n_cols)
        qbd = jnp.pad(qbd, ((0, 0), (0, 0), (0, pad_cols))).astype(BF16)
        pad_new = ((0, 0), (0, NEW_ROWS - seq_s * N_KV_HEADS), (0, 0))
        knew = jnp.pad(kv_s[0][l].reshape(nb_s, seq_s * N_KV_HEADS, QK_DIM), pad_new)
        vnew = jnp.pad(kv_s[1][l].reshape(nb_s, seq_s * N_KV_HEADS, V_DIM), pad_new)
        o_s = _sample_attention(page_table, lam, qbd, cache_k2, cache_v2, l, base_bias, pagevec,
                                knew, vnew, bias_new, subln_g[l].reshape(1, V_DIM), seq_s=seq_s,
                                out_scale=out_scale)
        o_s = o_s.reshape(nb_s, N_KV_HEADS, REP, seq_s, V_DIM).transpose(0, 3, 1, 2, 4).reshape(n_s, N_HEADS * V_DIM)
        prev = state_conv[l]
        zeros = jnp.zeros((nb_s, seq_s - 1, cd), F32)
        fix1 = jnp.concatenate([prev[:, 1:2], zeros], axis=1).reshape(n_s, cd)
        fix2 = jnp.concatenate([prev, zeros[:, 1:]], axis=1).reshape(n_s, cd)
        part_s, ga_s, z_s, zv_s = _branch_proj(ys, w_br, row2(gmlp_ln_g[l]), row2(gmlp_ln_b[l]), ws_s,
                                               bs_s, conv_w[l], pg, pc, (fix1, fix2), prompt=False,
                                               tm=tm_s, seq_len=seq_s, period=seq_s)
        x1s = _mix_proj(ys, o_s, ga_s, part_s, pa, wo, row2(ln1_g[l]), row2(ln1_b[l]), tm=tm_s, alpha=alpha)
        ys = mixer(x1s, tm_s)
        outs["cs"].append(z_s.reshape(nb_s, seq_s, cd)[:, seq_s - (CONV_WIDTH - 1):])
        outs["gs"].append(zv_s.reshape(nb_s, seq_s, -1))

    return (yp.reshape(nb_p, seq_p, d_model), ys.reshape(nb_s, seq_s, d_model),
            kv_p[0].reshape(depth, nb_p, seq_p, N_KV_HEADS, QK_DIM),
            kv_p[1].reshape(depth, nb_p, seq_p, N_KV_HEADS, V_DIM), jnp.stack(outs["cp"]),
            kv_s[0].reshape(depth, nb_s, seq_s, N_KV_HEADS, QK_DIM),
            kv_s[1].reshape(depth, nb_s, seq_s, N_KV_HEADS, V_DIM),
            jnp.stack(outs["cs"]), jnp.stack(outs["gs"]))
```

```python
import functools
import math

import jax
import jax.numpy as jnp
import numpy as np
from jax import lax
from jax.experimental import pallas as pl
from jax.experimental.pallas import tpu as pltpu

F32 = jnp.float32
BF16 = jnp.bfloat16

N_HEADS = 8
N_KV_HEADS = 4
REP = N_HEADS // N_KV_HEADS
HEAD_DIM = 64
QK_DIM = 2 * HEAD_DIM
V_DIM = 2 * HEAD_DIM
GMLP_GROUPS = 4
CHUNK = 128
CONV_WIDTH = 3
N_BRANCH = 3
LN_EPS = 1e-5
LOG2E = 1.4426950408889634
NEG_BIG = -1e30

LANES = 128
SUBLANES = 8
VMEM_LIMIT = 56 * 1024 * 1024

ATTN_BLOCK = 256
ATTN_UNROLL = 4
PAGE_LEAD = 2
PAGE_SLOTS = PAGE_LEAD + 1
PAGES_PER_STEP = 16
MOE_TILE = 1024
ONES_ROWS = 16
NEW_ROWS = 16


def _cparams(sem):
    return pltpu.CompilerParams(dimension_semantics=sem, vmem_limit_bytes=VMEM_LIMIT)


def _resident(shape):
    nd = len(shape)
    return pl.BlockSpec(shape, lambda *_: (0,) * nd, pipeline_mode=pl.Buffered(1))


def _layer_norm(v, g, b):
    mu = jnp.mean(v, axis=-1, keepdims=True)
    d = v - mu
    var = jnp.mean(d * d, axis=-1, keepdims=True)
    return d * lax.rsqrt(var + LN_EPS) * g + b


def _dot(a, b):
    return jnp.dot(a, b, preferred_element_type=F32)


def _qkv_kernel(x_ref, w_ref, *refs, prompt, tk, q_scale, n_alias):
    out_refs = refs[n_alias:]
    d_q = N_HEADS * QK_DIM
    d_k = N_KV_HEADS * QK_DIM
    tm = x_ref.shape[0]
    xb = x_ref[...].astype(BF16)
    q = _dot(xb, w_ref[:, 0:d_q]) * q_scale
    k = _dot(xb, w_ref[:, d_q:d_q + d_k])
    v = _dot(xb, w_ref[:, d_q + d_k:])
    kf_ref, vf_ref = out_refs[-2:]
    for g in range(N_KV_HEADS):
        rows = pl.ds(g, tm, stride=N_KV_HEADS)
        kf_ref[rows, :] = k[:, g * QK_DIM:(g + 1) * QK_DIM]
        vf_ref[rows, :] = v[:, g * V_DIM:(g + 1) * V_DIM]
    if prompt:
        q0_ref, q1_ref, kb_ref, vt_ref = out_refs[:4]
        first_half = (lax.broadcasted_iota(jnp.int32, q.shape, 1) % QK_DIM) < HEAD_DIM
        q0_ref[...] = jnp.where(first_half, q, 0.0).astype(BF16)
        q1_ref[...] = jnp.where(first_half, 0.0, q).astype(BF16)
        kb_ref[...] = k.astype(BF16)
        for g in range(N_KV_HEADS):
            for j in range(tm // tk):
                blk = v[j * tk:(j + 1) * tk, g * V_DIM:(g + 1) * V_DIM]
                ones = jnp.ones((ONES_ROWS, tk), F32)
                vt_ref[g, j] = jnp.concatenate([blk.T, ones], axis=0).astype(BF16)
    else:
        out_refs[0][...] = q


def _qkv_proj(x, w_qkv, kv_stacks, *, layer, depth, prompt, tm, seq_len, tk, q_scale):
    n, d = x.shape
    d_q = N_HEADS * QK_DIM
    d_kv = N_KV_HEADS * QK_DIM
    nt = n // tm
    row = lambda i: (i, 0)
    in_specs = [pl.BlockSpec((tm, d), row), _resident(w_qkv.shape)]
    if prompt:
        tps = seq_len // tm
        nb = n // seq_len
        out_shape = [
            jax.ShapeDtypeStruct((n, d_q), BF16), jax.ShapeDtypeStruct((n, d_q), BF16),
            jax.ShapeDtypeStruct((n, d_kv), BF16),
            jax.ShapeDtypeStruct((nb, N_KV_HEADS, seq_len // tk, V_DIM + ONES_ROWS, tk), BF16),
        ]
        out_specs = [
            pl.BlockSpec((tm, d_q), row), pl.BlockSpec((tm, d_q), row),
            pl.BlockSpec((tm, d_kv), row),
            pl.BlockSpec((None, N_KV_HEADS, tm // tk, V_DIM + ONES_ROWS, tk),
                         lambda i: (i // tps, 0, i % tps, 0, 0)),
        ]
    else:
        out_shape = [jax.ShapeDtypeStruct((n, d_q), F32)]
        out_specs = [pl.BlockSpec((tm, d_q), row)]
    stack_shape = jax.ShapeDtypeStruct((depth, n * N_KV_HEADS, QK_DIM), F32)
    stack_spec = pl.BlockSpec((None, tm * N_KV_HEADS, QK_DIM), lambda i: (layer, i, 0))
    n_out = len(out_shape)
    out_shape += [stack_shape, stack_shape]
    out_specs += [stack_spec, stack_spec]
    in_specs += [pl.BlockSpec(memory_space=pl.ANY)] * 2
    args = [x, w_qkv, *kv_stacks]
    aliases = {2: n_out, 3: n_out + 1}
    return pl.pallas_call(
        functools.partial(_qkv_kernel, prompt=prompt, tk=tk, q_scale=q_scale, n_alias=len(aliases)),
        grid=(nt,), in_specs=in_specs, out_specs=out_specs, out_shape=out_shape,
        input_output_aliases=aliases,
        compiler_params=_cparams(("arbitrary",)), name="qkv_proj",
    )(*args)


def _prompt_attn_kernel(lam_ref, sig_ref, q0_ref, q1_ref, k_ref, vt_ref, boff_ref, bdiag_ref,
                        gain_ref, o_ref, acc_ref, s_ref, *, tq, tk, out_scale):
    g = pl.program_id(1)
    qi = pl.program_id(2)
    n_state = REP * 2

    def scores(j, h):
        r, m = divmod(h, 2)
        kj = k_ref[pl.ds(pl.multiple_of(j * tk, tk), tk), :]
        qh = (q0_ref if m == 0 else q1_ref)[:, r * QK_DIM:(r + 1) * QK_DIM]
        return lax.dot_general(kj, qh, (((1,), (1,)), ((), ())), preferred_element_type=F32)

    def block(j, bias_ref, carry, prefetch):
        m_all = carry
        vtj = vt_ref[j]
        dist_blocks = (qi - j).astype(F32) * float(tq)
        m_out = []
        for h in range(n_state):
            r = h // 2
            c_j = -sig_ref[g * REP + r] * dist_blocks
            s_t = s_ref[h] + bias_ref[r]
            if prefetch:
                s_ref[h] = scores(j + 1, h)
            m_new = jnp.maximum(m_all[h], jnp.max(s_t, axis=0, keepdims=True) + c_j)
            alpha = jnp.exp2(m_all[h] - m_new)
            p_t = jnp.exp2(s_t - (m_new - c_j))
            m_out.append(m_new)
            acc_ref[h] = acc_ref[h] * alpha + _dot(vtj, p_t.astype(BF16))
        return tuple(m_out)

    acc_ref[...] = jnp.zeros_like(acc_ref)
    for h in range(n_state):
        s_ref[h] = scores(0, h)
    init = tuple(jnp.full((1, tq), NEG_BIG, F32) for _ in range(n_state))
    def group(jj, c):
        for u in range(ATTN_UNROLL):
            c = block(ATTN_UNROLL * jj + u, boff_ref, c, True)
        return c

    def pair(jj, c):
        c = block(n_grouped + 2 * jj, boff_ref, c, True)
        return block(n_grouped + 2 * jj + 1, boff_ref, c, True)

    n_grouped = (qi // ATTN_UNROLL) * ATTN_UNROLL
    n_paired = n_grouped + ((qi - n_grouped) // 2) * 2
    carry = lax.fori_loop(0, qi // ATTN_UNROLL, group, init)
    carry = lax.fori_loop(0, (qi - n_grouped) // 2, pair, carry)
    carry = lax.fori_loop(n_paired, qi, lambda j, c: block(j, boff_ref, c, True), carry)
    block(qi, bdiag_ref, carry, False)

    lam = lam_ref[0]
    gain = gain_ref[...] * out_scale
    for r in range(REP):
        a0, a1 = acc_ref[2 * r], acc_ref[2 * r + 1]
        o_t = a0[:V_DIM] / a0[V_DIM:V_DIM + 1] - lam * (a1[:V_DIM] / a1[V_DIM:V_DIM + 1])
        ms = jnp.mean(o_t * o_t, axis=0, keepdims=True)
        o_t = o_t * lax.rsqrt(ms + LN_EPS) * gain
        o_ref[:, r * V_DIM:(r + 1) * V_DIM] = o_t.T.astype(o_ref.dtype)


def _prompt_attention(lam, sigma, q0, q1, kb, vt, bias_off, bias_diag, gain, *, seq_len, out_scale):
    n = q0.shape[0]
    nb = n // seq_len
    tq = tk = ATTN_BLOCK
    nq = seq_len // tq
    smem = pl.BlockSpec(memory_space=pltpu.SMEM)
    qspec = pl.BlockSpec((tq, REP * QK_DIM), lambda b, g, i: (b * nq + i, g))
    return pl.pallas_call(
        functools.partial(_prompt_attn_kernel, tq=tq, tk=tk, out_scale=out_scale),
        grid=(nb, N_KV_HEADS, nq),
        in_specs=[
            smem, smem, qspec, qspec,
            pl.BlockSpec((seq_len, QK_DIM), lambda b, g, i: (b, g)),
            pl.BlockSpec((None, None, seq_len // tk, V_DIM + ONES_ROWS, tk),
                         lambda b, g, i: (b, g, 0, 0, 0)),
            pl.BlockSpec((None, REP, tk, tq), lambda b, g, i: (g, 0, 0, 0)),
            pl.BlockSpec((None, REP, tk, tq), lambda b, g, i: (g, 0, 0, 0)),
            pl.BlockSpec((V_DIM, 1), lambda b, g, i: (0, 0)),
        ],
        out_specs=pl.BlockSpec((tq, REP * V_DIM), lambda b, g, i: (b * nq + i, g)),
        out_shape=jax.ShapeDtypeStruct((n, N_HEADS * V_DIM), BF16),
        scratch_shapes=[pltpu.VMEM((REP * 2, V_DIM + ONES_ROWS, tq), F32),
                        pltpu.VMEM((REP * 2, tk, tq), F32)],
        compiler_params=_cparams(("arbitrary", "arbitrary", "arbitrary")), name="prompt_attn",
    )(lam, sigma, q0, q1, kb, vt, bias_off, bias_diag, gain)


def _sample_attn_kernel(pt_ref, lam_ref, qbd_ref, k_hbm, v_hbm, base_ref, pagevec_ref, knew_ref,
                        vnew_ref, bnew_ref, gain_ref, o_ref, m_ref, l_ref, acc_ref, sa_ref, sb_ref,
                        kbuf, vbuf, sem, *, n_pages_step, n_chunks, layer, out_scale):
    P = n_pages_step
    b = pl.program_id(0)
    c = pl.program_id(1)
    n_c = n_chunks + 1
    step_id = b * n_c + c
    n_steps = pl.num_programs(0) * n_c
    rows_g = o_ref.shape[1]
    half = N_KV_HEADS * rows_g

    def k_copy(bb, cc, slot, p):
        return pltpu.make_async_copy(k_hbm.at[layer, pt_ref[bb, cc * P + p]], kbuf.at[slot, p],
                                     sem.at[0, slot])

    def v_copy(bb, cc, slot, p):
        return pltpu.make_async_copy(v_hbm.at[layer, pt_ref[bb, cc * P + p]], vbuf.at[slot, p],
                                     sem.at[1, slot])

    def request(s):
        bb = s // n_c
        cc = s % n_c
        slot = s % PAGE_SLOTS

        @pl.when((s < n_steps) & (cc < n_chunks))
        def _():
            for p in range(P):
                k_copy(bb, cc, slot, p).start()

        @pl.when((s < n_steps) & (cc >= 1))
        def _():
            for p in range(P):
                v_copy(bb, cc - 1, slot, p).start()

    @pl.when(step_id == 0)
    def _():
        for s in range(PAGE_LEAD):
            request(jnp.int32(s))

    request(step_id + PAGE_LEAD)
    slot = step_id % PAGE_SLOTS

    @pl.when(c < n_chunks)
    def _():
        for p in range(P):
            k_copy(b, c, slot, p).wait()

    @pl.when(c >= 1)
    def _():
        for p in range(P):
            v_copy(b, c - 1, slot, p).wait()

    @pl.when(c == 0)
    def _():
        m_ref[...] = jnp.full_like(m_ref, NEG_BIG)
        l_ref[...] = jnp.zeros_like(l_ref)
        acc_ref[...] = jnp.zeros_like(acc_ref)

    qbd = qbd_ref[...]

    def col_bcast(row):
        return jnp.transpose(jnp.broadcast_to(row, (LANES, LANES)))

    def update(s_tiles, v_tiles):
        m_old = m_ref[...]
        mx = s_tiles[0].max(axis=0, keepdims=True)
        for s_t in s_tiles[1:]:
            mx = jnp.maximum(mx, s_t.max(axis=0, keepdims=True))
        m_new = jnp.maximum(m_old, mx)
        alpha = jnp.exp2(m_old - m_new)
        lsum = jnp.zeros_like(m_old)
        pv = None
        for s_t, v_t in zip(s_tiles, v_tiles):
            p_t = jnp.exp2(s_t - m_new)
            lsum = lsum + jnp.sum(p_t, axis=0, keepdims=True)
            part = lax.dot_general(p_t.astype(BF16), v_t, (((0,), (0,)), ((), ())),
                                   preferred_element_type=F32)
            pv = part if pv is None else pv + part
        m_ref[...] = m_new
        l_ref[...] = alpha * l_ref[...] + lsum
        acc_ref[...] = acc_ref[...] * col_bcast(alpha) + pv

    def step(s_write, s_read):
        if s_write is not None:
            base = base_ref[...]
            page0 = (c * P).astype(F32)
            for p in range(P):
                s_write[p] = (_dot(kbuf[slot, p].astype(BF16), qbd) + base
                              + pagevec_ref[...] * (page0 + float(p)))
        if s_read is not None:
            update([s_read[p] for p in range(P)], [vbuf[slot, p].astype(BF16) for p in range(P)])

    last_read = sb_ref if n_chunks % 2 == 0 else sa_ref

    @pl.when(c == 0)
    def _():
        step(sa_ref, None)

    @pl.when((c % 2 == 1) & (c < n_chunks))
    def _():
        step(sb_ref, sa_ref)

    @pl.when((c % 2 == 0) & (c > 0) & (c < n_chunks))
    def _():
        step(sa_ref, sb_ref)

    @pl.when(c == n_chunks)
    def _():
        step(None, last_read)
        s_new = _dot(knew_ref[...].astype(BF16), qbd) + bnew_ref[...]
        update([s_new], [vnew_ref[...].astype(BF16)])
        l_col = col_bcast(l_ref[...])
        acc = acc_ref[...]
        o = acc[:half] / l_col[:half] - lam_ref[0] * (acc[half:2 * half] / l_col[half:2 * half])
        ms = jnp.mean(o * o, axis=-1, keepdims=True)
        o = o * lax.rsqrt(ms + LN_EPS) * (gain_ref[...] * out_scale)
        for g in range(N_KV_HEADS):
            o_ref[g] = o[g * rows_g:(g + 1) * rows_g].astype(o_ref.dtype)


def _sample_attention(page_table, lam, qbd, cache_k, cache_v, layer, base, pagevec, knew, vnew,
                      bias_new, gain, *, seq_s, out_scale):
    nb, n_pages = page_table.shape
    P = PAGES_PER_STEP
    page_shape = cache_k.shape[2:]
    rows_g = REP * seq_s
    smem = pl.BlockSpec(memory_space=pltpu.SMEM)

    n_chunks = n_pages // P
    hbm = pl.BlockSpec(memory_space=pl.ANY)
    ring = (PAGE_SLOTS, P, page_shape[0], LANES)
    const2 = lambda b, c, pt: (0, 0)
    per_seq = lambda b, c, pt: (b, 0, 0)
    grid_spec = pltpu.PrefetchScalarGridSpec(
        num_scalar_prefetch=1,
        grid=(nb, n_chunks + 1),
        in_specs=[smem, pl.BlockSpec((None,) + qbd.shape[1:], per_seq), hbm, hbm]
        + [pl.BlockSpec(base.shape, const2), pl.BlockSpec(pagevec.shape, const2),
           pl.BlockSpec((None,) + knew.shape[1:], per_seq),
           pl.BlockSpec((None,) + vnew.shape[1:], per_seq),
           pl.BlockSpec(bias_new.shape, const2), pl.BlockSpec(gain.shape, const2)],
        out_specs=pl.BlockSpec((None, N_KV_HEADS, rows_g, V_DIM), lambda b, c, pt: (b, 0, 0, 0)),
        scratch_shapes=[pltpu.VMEM((1, LANES), F32), pltpu.VMEM((1, LANES), F32),
                        pltpu.VMEM((LANES, V_DIM), F32),
                        pltpu.VMEM((P, page_shape[0], LANES), F32),
                        pltpu.VMEM((P, page_shape[0], LANES), F32),
                        pltpu.VMEM(ring, F32), pltpu.VMEM(ring, F32),
                        pltpu.SemaphoreType.DMA((2, PAGE_SLOTS))],
    )
    return pl.pallas_call(
        functools.partial(_sample_attn_kernel, n_pages_step=P, n_chunks=n_chunks, layer=layer,
                          out_scale=out_scale),
        grid_spec=grid_spec,
        out_shape=jax.ShapeDtypeStruct((nb, N_KV_HEADS, rows_g, V_DIM), BF16),
        compiler_params=_cparams(("arbitrary", "arbitrary")), name="sample_attn",
    )(page_table, lam, qbd, cache_k, cache_v, base, pagevec, knew, vnew, bias_new, gain)


def _branch_kernel(x_ref, w_ref, lng_ref, lnb_ref, ws_ref, bs_ref, cw_ref, pg_ref, pc_ref, *refs,
                   prompt, tiles_per_seq, period):
    d_model = x_ref.shape[1]
    gd = pg_ref.shape[0]
    cd = pc_ref.shape[0]
    tm = x_ref.shape[0]
    if prompt:
        part_ref, ga_ref, ztail_ref, carry_ref = refs
    else:
        fix1_ref, fix2_ref, part_ref, ga_ref, z_ref, zv_ref = refs
    xb = x_ref[...].astype(BF16)

    def proj(lo, width):
        return _dot(xb, w_ref[:, lo:lo + width])

    z_u = jax.nn.gelu(proj(0, gd))
    z_v = _layer_norm(jax.nn.gelu(proj(gd, gd)), lng_ref[...], lnb_ref[...])
    if not prompt:
        zv_ref[...] = z_v
    gw = gd // GMLP_GROUPS
    z_vb = z_v.astype(BF16)
    rows = []
    for n in range(tm // CHUNK):
        cols = []
        for g in range(GMLP_GROUPS):
            blk = z_vb[n * CHUNK:(n + 1) * CHUNK, g * gw:(g + 1) * gw]
            cols.append(_dot(ws_ref[g], blk) + bs_ref[g])
        rows.append(jnp.concatenate(cols, axis=1))
    s = jnp.concatenate(rows, axis=0)
    y_g = _dot((z_u * s).astype(BF16), pg_ref[...])

    off = 2 * gd
    c_b = proj(off, cd)
    z = proj(off + cd, cd) * proj(off + 2 * cd, cd)
    zr1 = pltpu.roll(z, 1, 0)
    zr2 = pltpu.roll(z, 2, 0)
    if prompt:
        @pl.when(pl.program_id(0) % tiles_per_seq == 0)
        def _():
            carry_ref[...] = jnp.zeros_like(carry_ref)

        prev = carry_ref[...]
        r8 = lax.broadcasted_iota(jnp.int32, (SUBLANES, cd), 0)
        top1 = jnp.where(r8 < 1, pltpu.roll(prev, 1, 0), zr1[:SUBLANES])
        top2 = jnp.where(r8 < 2, pltpu.roll(prev, 2, 0), zr2[:SUBLANES])
        zm1 = jnp.concatenate([top1, zr1[SUBLANES:]], axis=0)
        zm2 = jnp.concatenate([top2, zr2[SUBLANES:]], axis=0)
        tail = z[tm - SUBLANES:]
        carry_ref[...] = tail
        ztail_ref[...] = tail
    else:
        pos = lax.broadcasted_iota(jnp.int32, z.shape, 0) % period
        zm1 = jnp.where(pos >= 1, zr1, fix1_ref[...])
        zm2 = jnp.where(pos >= 2, zr2, fix2_ref[...])
        z_ref[...] = z
    cw = cw_ref[...]
    y = cw[0:1] * zm2 + cw[1:2] * zm1 + cw[2:3] * z
    y_c = _dot((c_b * y).astype(BF16), pc_ref[...])

    off = 2 * gd + 3 * cd
    ga_ref[...] = jax.nn.sigmoid(proj(off, d_model)).astype(ga_ref.dtype)
    part_ref[...] = (jax.nn.sigmoid(proj(off + d_model, d_model)) * y_g
                     + jax.nn.sigmoid(proj(off + 2 * d_model, d_model)) * y_c)


def _branch_proj(x, w_br, ln_g, ln_b, ws_mat, bs_mat, conv_w, p_gmlp, p_conv, fixes, *,
                 prompt, tm, seq_len, period):
    n, d = x.shape
    gd, cd = p_gmlp.shape[0], p_conv.shape[0]
    nt = n // tm
    row = lambda i: (i, 0)
    in_specs = [pl.BlockSpec((tm, d), row), _resident(w_br.shape), _resident(ln_g.shape),
                _resident(ln_b.shape), _resident(ws_mat.shape), _resident(bs_mat.shape),
                _resident(conv_w.shape), _resident(p_gmlp.shape), _resident(p_conv.shape)]
    args = [x, w_br, ln_g, ln_b, ws_mat, bs_mat, conv_w, p_gmlp, p_conv]
    out_shape = [jax.ShapeDtypeStruct((n, d), F32), jax.ShapeDtypeStruct((n, d), BF16)]
    out_specs = [pl.BlockSpec((tm, d), row), pl.BlockSpec((tm, d), row)]
    scratch = []
    if prompt:
        out_shape.append(jax.ShapeDtypeStruct((nt * SUBLANES, cd), F32))
        out_specs.append(pl.BlockSpec((SUBLANES, cd), row))
        scratch.append(pltpu.VMEM((SUBLANES, cd), F32))
    else:
        in_specs += [pl.BlockSpec((tm, cd), row), pl.BlockSpec((tm, cd), row)]
        args += list(fixes)
        out_shape += [jax.ShapeDtypeStruct((n, cd), F32), jax.ShapeDtypeStruct((n, gd), F32)]
        out_specs += [pl.BlockSpec((tm, cd), row), pl.BlockSpec((tm, gd), row)]
    return pl.pallas_call(
        functools.partial(_branch_kernel, prompt=prompt,
                          tiles_per_seq=max(seq_len // tm, 1), period=period),
        grid=(nt,), in_specs=in_specs, out_specs=out_specs, out_shape=out_shape,
        scratch_shapes=scratch,
        compiler_params=_cparams(("arbitrary",)), name="branch_proj",
    )(*args)


def _mix_kernel(x_ref, o_ref, ga_ref, part_ref, pa_ref, wo_ref, g_ref, b_ref, out_ref, *, alpha):
    y_a = _dot(o_ref[...], pa_ref[...])
    mix = ga_ref[...].astype(F32) * y_a + part_ref[...]
    h = _dot(mix.astype(BF16), wo_ref[...])
    out_ref[...] = _layer_norm(alpha * x_ref[...] + h, g_ref[...], b_ref[...])


def _mix_proj(x, o, ga, part, p_attn, w_o, ln_g, ln_b, *, tm, alpha):
    n, d = x.shape
    row = lambda i: (i, 0)
    return pl.pallas_call(
        functools.partial(_mix_kernel, alpha=alpha),
        grid=(n // tm,),
        in_specs=[pl.BlockSpec((tm, d), row), pl.BlockSpec((tm, o.shape[1]), row),
                  pl.BlockSpec((tm, d), row), pl.BlockSpec((tm, d), row),
                  _resident(p_attn.shape), _resident(w_o.shape),
                  _resident(ln_g.shape), _resident(ln_b.shape)],
        out_specs=pl.BlockSpec((tm, d), row),
        out_shape=jax.ShapeDtypeStruct((n, d), F32),
        compiler_params=_cparams(("arbitrary",)), name="mix_proj",
    )(x, o, ga, part, p_attn, w_o, ln_g, ln_b)


def _ffn_kernel(x_ref, wg_ref, wu_ref, wd_ref, g_ref, b_ref, out_ref, *, alpha, ff_chunk):
    x = x_ref[...]
    xb = x.astype(BF16)
    f = None
    for c0 in range(0, wg_ref.shape[1], ff_chunk):
        h = jax.nn.silu(_dot(xb, wg_ref[:, c0:c0 + ff_chunk])) * _dot(xb, wu_ref[:, c0:c0 + ff_chunk])
        part = _dot(h.astype(BF16), wd_ref[c0:c0 + ff_chunk, :])
        f = part if f is None else f + part
    out_ref[...] = _layer_norm(alpha * x + f, g_ref[...], b_ref[...])


def _ff_chunk(d_ff):
    for c in (1408, 1024, 512, 256, 128):
        if d_ff % c == 0:
            return c
    return d_ff


def _dense_ffn(x, wg, wu, wd, ln_g, ln_b, *, tm, alpha):
    n, d = x.shape
    row = lambda i: (i, 0)
    return pl.pallas_call(
        functools.partial(_ffn_kernel, alpha=alpha, ff_chunk=_ff_chunk(wg.shape[1])),
        grid=(n // tm,),
        in_specs=[pl.BlockSpec((tm, d), row), _resident(wg.shape), _resident(wu.shape),
                  _resident(wd.shape), _resident(ln_g.shape), _resident(ln_b.shape)],
        out_specs=pl.BlockSpec((tm, d), row),
        out_shape=jax.ShapeDtypeStruct((n, d), F32),
        compiler_params=_cparams(("arbitrary",)), name="dense_ffn",
    )(x, wg, wu, wd, ln_g, ln_b)


def _router_kernel(x_ref, r_ref, tri_ref, comb_ref, rank_ref, cnt_ref, *, n_experts):
    x, r = x_ref[...], r_ref[...]
    x_hi, r_hi = x.astype(BF16), r.astype(BF16)
    x_lo = (x - x_hi.astype(F32)).astype(BF16)
    r_lo = (r - r_hi.astype(F32)).astype(BF16)
    logits = _dot(x_hi, r_hi) + (_dot(x_lo, r_hi) + _dot(x_hi, r_lo))
    lt = logits.T[:n_experts]
    row = lax.broadcasted_iota(jnp.int32, lt.shape, 0)
    m1 = jnp.max(lt, axis=0, keepdims=True)
    i1 = jnp.min(jnp.where(lt == m1, row, n_experts), axis=0, keepdims=True)
    rest = jnp.where(row == i1, -jnp.inf, lt)
    m2 = jnp.max(rest, axis=0, keepdims=True)
    i2 = jnp.min(jnp.where(rest == m2, row, n_experts), axis=0, keepdims=True)
    t = jnp.exp(m2 - m1)
    w1 = 1.0 / (1.0 + t)
    comb = jnp.where(row == i1, w1, 0.0) + jnp.where(row == i2, t * w1, 0.0)
    routed = jnp.where(comb > 0.0, 1.0, 0.0)
    lhs = jnp.concatenate([routed, jnp.zeros_like(routed)], axis=0).astype(BF16)
    comb_ref[...] = comb
    rank_ref[...] = _dot(lhs, tri_ref[...])[:n_experts]
    cnt_ref[...] = jnp.broadcast_to(jnp.sum(routed, axis=1, keepdims=True), cnt_ref.shape)


def _moe_kernel(cnt_ref, x_ref, comb_ref, rank_ref, wg_ref, wu_ref, wd_ref, g_ref, b_ref, out_ref,
                xb_ref, xs_ref, y_ref, *, alpha, rows):
    i = pl.program_id(0)
    e = pl.program_id(1)
    c = pl.program_id(2)
    last_c = pl.num_programs(2) - 1
    t = x_ref.shape[0]
    n_blk = (cnt_ref[i, e] + rows - 1) // rows
    comb_row = comb_ref[pl.ds(e, 1), :]
    rank_row = rank_ref[pl.ds(e, 1), :]

    @pl.when((e == 0) & (c == 0))
    def _():
        xb_ref[...] = x_ref[...].astype(BF16)
        out_ref[...] = jnp.zeros_like(out_ref)

    def block_rows(b):
        return pl.ds(pl.multiple_of(b * rows, rows), rows)

    def selection(b, weighted):
        slot = (lax.broadcasted_iota(jnp.int32, (rows, t), 0) + b * rows).astype(F32)
        hit = (rank_row == slot) & (comb_row > 0.0)
        return jnp.where(hit, comb_row if weighted else 1.0, 0.0).astype(BF16)

    def expert(b):
        xs = xs_ref[block_rows(b), :]
        h = jax.nn.silu(_dot(xs, wg_ref[...])) * _dot(xs, wu_ref[...])
        return _dot(h.astype(BF16), wd_ref[...])

    def for_blocks(body):
        def step(b, carry):
            body(b)
            return carry
        lax.fori_loop(0, n_blk, step, 0)

    @pl.when(c == 0)
    def _():
        def gather_and_run(b):
            xs_ref[block_rows(b), :] = _dot(selection(b, False), xb_ref[...]).astype(BF16)
            y_ref[block_rows(b), :] = expert(b)
        for_blocks(gather_and_run)

    @pl.when(c > 0)
    def _():
        def run(b):
            y_ref[block_rows(b), :] += expert(b)
        for_blocks(run)

    @pl.when(c == last_c)
    def _():
        def scatter(b):
            out_ref[...] += lax.dot_general(selection(b, True), y_ref[block_rows(b), :].astype(BF16),
                                            (((0,), (0,)), ((), ())), preferred_element_type=F32)
        for_blocks(scatter)

    @pl.when((e == pl.num_programs(1) - 1) & (c == last_c))
    def _():
        out_ref[...] = _layer_norm(alpha * x_ref[...] + out_ref[...], g_ref[...], b_ref[...])


def _moe_block_rows(tm, n_experts, top_k=2):
    mean = tm * top_k // n_experts
    return min(tm, -(-(mean + 32) // 32) * 32)


def _moe_ffn(x, router_pad, wg, wu, wd, ln_g, ln_b, *, tm, alpha, n_experts):
    n, d = x.shape
    nt = n // tm
    d_ff = wg.shape[2]
    fc = _ff_chunk(d_ff)
    wg, wu, wd = wg.astype(BF16), wu.astype(BF16), wd.astype(BF16)
    rows = _moe_block_rows(tm, n_experts)
    cap = -(-tm // rows) * rows
    tri = jnp.asarray(np.arange(tm)[:, None] < np.arange(tm)[None, :], dtype=BF16)
    comb, rank, cnt = pl.pallas_call(
        functools.partial(_router_kernel, n_experts=n_experts),
        grid=(nt,),
        in_specs=[pl.BlockSpec((tm, d), lambda i: (i, 0)), _resident(router_pad.shape),
                  _resident(tri.shape)],
        out_specs=[pl.BlockSpec((n_experts, tm), lambda i: (0, i)),
                   pl.BlockSpec((n_experts, tm), lambda i: (0, i)),
                   pl.BlockSpec((None, n_experts, LANES), lambda i: (i, 0, 0))],
        out_shape=[jax.ShapeDtypeStruct((n_experts, n), F32), jax.ShapeDtypeStruct((n_experts, n), F32),
                   jax.ShapeDtypeStruct((nt, n_experts, LANES), F32)],
        compiler_params=_cparams(("arbitrary",)), name="moe_router",
    )(x, router_pad, tri)
    counts = cnt[:, :, 0].astype(jnp.int32)
    row = lambda i, e, c, cnt: (i, 0)
    col = lambda i, e, c, cnt: (0, i)
    const = lambda i, e, c, cnt: (0, 0)
    grid_spec = pltpu.PrefetchScalarGridSpec(
        num_scalar_prefetch=1,
        grid=(nt, n_experts, d_ff // fc),
        in_specs=[pl.BlockSpec((tm, d), row),
                  pl.BlockSpec((n_experts, tm), col), pl.BlockSpec((n_experts, tm), col),
                  pl.BlockSpec((None, d, fc), lambda i, e, c, cnt: (e, 0, c)),
                  pl.BlockSpec((None, d, fc), lambda i, e, c, cnt: (e, 0, c)),
                  pl.BlockSpec((None, fc, d), lambda i, e, c, cnt: (e, c, 0)),
                  pl.BlockSpec(ln_g.shape, const), pl.BlockSpec(ln_b.shape, const)],
        out_specs=pl.BlockSpec((tm, d), row),
        scratch_shapes=[pltpu.VMEM((tm, d), BF16), pltpu.VMEM((cap, d), BF16),
                        pltpu.VMEM((cap, d), F32)],
    )
    return pl.pallas_call(
        functools.partial(_moe_kernel, alpha=alpha, rows=rows),
        grid_spec=grid_spec,
        out_shape=jax.ShapeDtypeStruct((n, d), F32),
        compiler_params=_cparams(("arbitrary", "arbitrary", "arbitrary")), name="moe_ffn",
    )(counts, x, comb, rank, wg, wu, wd, ln_g, ln_b)


def _token_tile(n, cap):
    t = cap
    while n % t:
        t //= 2
    return t


def kernel(x_prompt, x_sample, cache_k, cache_v, state_conv, page_table, w_in, lam_params, subln_g, gmlp_ln_g, gmlp_ln_b, gmlp_ws, gmlp_bs, conv_w, p_attn, p_gmlp, p_conv, w_o, ln1_g, ln1_b, ln2_g, ln2_b, ffn_gate, ffn_up, ffn_down, router, moe_gate, moe_up, moe_down):
    depth = w_in.shape[0]
    nb_p, seq_p, d_model = x_prompt.shape
    nb_s, seq_s, _ = x_sample.shape
    n_pool, page_size = cache_k.shape[1], cache_k.shape[2]
    n_pages = page_table.shape[1]
    past_len = n_pages * page_size
    n_experts = router.shape[2]
    alpha = (2 * depth) ** 0.25
    d_q = N_HEADS * QK_DIM
    d_kv = N_KV_HEADS * QK_DIM
    d_qkv = d_q + 2 * d_kv
    cd = p_conv.shape[1]
    q_scale = LOG2E * HEAD_DIM ** -0.5

    n_p = nb_p * seq_p
    n_s = nb_s * seq_s
    tm_p = _token_tile(seq_p, 512)
    tm_s = n_s
    tq = ATTN_BLOCK

    f32 = np.float32
    sigma_np = (np.exp2(-8.0 * np.arange(1, N_HEADS + 1) / N_HEADS) * LOG2E).astype(f32)
    sigma = jnp.asarray(sigma_np)
    rel = np.arange(tq, dtype=f32)[None, :] - np.arange(tq, dtype=f32)[:, None]
    bias_off_np = (-sigma_np.reshape(N_KV_HEADS, REP, 1, 1) * rel).astype(f32)
    bias_off = jnp.asarray(bias_off_np)
    bias_diag = jnp.asarray(np.where(rel >= 0, bias_off_np, f32(NEG_BIG)).astype(f32))

    n_cols = 2 * N_KV_HEADS * REP * seq_s
    pad_cols = LANES - n_cols
    col_shape = (2, N_KV_HEADS, REP, seq_s)
    pad1 = lambda a: np.pad(np.broadcast_to(a, col_shape).reshape(n_cols), (0, pad_cols))
    sig_cols = pad1(sigma_np.reshape(1, N_KV_HEADS, REP, 1))[None, :]
    t_cols = pad1(np.arange(seq_s, dtype=f32))[None, :]
    g_cols = pad1(np.arange(N_KV_HEADS).reshape(1, N_KV_HEADS, 1, 1))[None, :]
    row_id = np.arange(page_size * N_KV_HEADS)[:, None]
    key_row = (row_id // N_KV_HEADS).astype(f32)
    base_bias = jnp.asarray(np.where((row_id % N_KV_HEADS) == g_cols,
                                     -sig_cols * (f32(past_len) + t_cols - key_row), f32(NEG_BIG)).astype(f32))
    pagevec = jnp.asarray((sig_cols * f32(page_size)).astype(f32))
    new_id = np.arange(NEW_ROWS)[:, None]
    new_row = (new_id // N_KV_HEADS).astype(f32)
    new_ok = ((new_id % N_KV_HEADS) == g_cols) & (new_row <= t_cols) & (new_row < seq_s)
    bias_new = jnp.asarray(np.where(new_ok, -sig_cols * (t_cols - new_row), f32(NEG_BIG)).astype(f32))
    cache_k2 = cache_k.reshape(depth, n_pool, page_size * N_KV_HEADS, QK_DIM)
    cache_v2 = cache_v.reshape(depth, n_pool, page_size * N_KV_HEADS, V_DIM)

    ws_tril = jnp.tril(gmlp_ws)
    eye_s = jnp.eye(CHUNK // seq_s, dtype=F32)

    row2 = lambda a: a.reshape(1, -1)
    router_pad = jnp.pad(router, ((0, 0), (0, 0), (0, LANES - n_experts)))

    yp = x_prompt.reshape(n_p, d_model)
    ys = x_sample.reshape(n_s, d_model)
    outs = {k: [] for k in ("cp", "cs", "gs")}
    kv_p = [jnp.zeros((depth, n_p * N_KV_HEADS, QK_DIM), F32)] * 2
    kv_s = [jnp.zeros((depth, n_s * N_KV_HEADS, QK_DIM), F32)] * 2
    for l in range(depth):
        lam_init = 0.8 - 0.6 * math.exp(-0.3 * l)
        lp = lam_params[l].astype(F32)
        lam = (jnp.exp(jnp.sum(lp[0] * lp[1])) - jnp.exp(jnp.sum(lp[2] * lp[3])) + lam_init).reshape(1)
        out_scale = 1.0 - lam_init
        w_l = w_in[l].astype(BF16)
        w_qkv, w_br = w_l[:, :d_qkv], w_l[:, d_qkv:]
        pa, pg, pc, wo = (p_attn[l].astype(BF16), p_gmlp[l].astype(BF16),
                          p_conv[l].astype(BF16), w_o[l].astype(BF16))
        ws_p = ws_tril[l].astype(BF16)
        bs_p = jnp.broadcast_to(gmlp_bs[l][:, :, None], (GMLP_GROUPS, CHUNK, LANES))
        ws_s = jnp.einsum("ab,gij->gaibj", eye_s, ws_tril[l][:, :seq_s, :seq_s]).reshape(
            GMLP_GROUPS, CHUNK, CHUNK).astype(BF16)
        bs_s = jnp.broadcast_to(jnp.tile(gmlp_bs[l][:, :seq_s], (1, CHUNK // seq_s))[:, :, None],
                                (GMLP_GROUPS, CHUNK, LANES))

        def mixer(x, tm):
            if l % 2 == 0:
                i = l // 2
                return _dense_ffn(x, ffn_gate[i].astype(BF16), ffn_up[i].astype(BF16),
                                  ffn_down[i].astype(BF16), row2(ln2_g[l]), row2(ln2_b[l]),
                                  tm=tm, alpha=alpha)
            i = l // 2
            return _moe_ffn(x, router_pad[i], moe_gate[i], moe_up[i], moe_down[i],
                            row2(ln2_g[l]), row2(ln2_b[l]),
                            tm=_token_tile(x.shape[0], MOE_TILE), alpha=alpha, n_experts=n_experts)

        q0, q1, kb, vt, *kv_p = _qkv_proj(yp, w_qkv, kv_p, layer=l, depth=depth, prompt=True, tm=tm_p,
                                          seq_len=seq_p, tk=tq, q_scale=q_scale)
        o = _prompt_attention(lam, sigma, q0, q1, kb, vt, bias_off, bias_diag,
                              subln_g[l].reshape(V_DIM, 1), seq_len=seq_p, out_scale=out_scale)
        part, ga, ztail = _branch_proj(yp, w_br, row2(gmlp_ln_g[l]), row2(gmlp_ln_b[l]), ws_p, bs_p,
                                       conv_w[l], pg, pc, None, prompt=True, tm=tm_p,
                                       seq_len=seq_p, period=1)
        x1 = _mix_proj(yp, o, ga, part, pa, wo, row2(ln1_g[l]), row2(ln1_b[l]), tm=tm_p, alpha=alpha)
        yp = mixer(x1, tm_p)
        tiles_per_seq = seq_p // tm_p
        outs["cp"].append(ztail.reshape(nb_p, tiles_per_seq, SUBLANES, cd)[:, -1, SUBLANES - (CONV_WIDTH - 1):])

        qs, *kv_s = _qkv_proj(ys, w_qkv, kv_s, layer=l, depth=depth, prompt=False, tm=tm_s,
                              seq_len=seq_s, tk=tq, q_scale=q_scale)
        q6 = qs.reshape(nb_s, seq_s, N_KV_HEADS, REP, 2, HEAD_DIM)
        eye_m = jnp.eye(2, dtype=F32)
        qbd = jnp.einsum("btgrmd,mn->bmdngrt", q6, eye_m).reshape(nb_s, QK_DIM, n_cols)
        qbd = jnp.pad(qbd, ((0, 0), (0, 0), (0, pad_cols))).astype(BF16)
        pad_new = ((0, 0), (0, NEW_ROWS - seq_s * N_KV_HEADS), (0, 0))
        knew = jnp.pad(kv_s[0][l].reshape(nb_s, seq_s * N_KV_HEADS, QK_DIM), pad_new)
        vnew = jnp.pad(kv_s[1][l].reshape(nb_s, seq_s * N_KV_HEADS, V_DIM), pad_new)
        o_s = _sample_attention(page_table, lam, qbd, cache_k2, cache_v2, l, base_bias, pagevec,
                                knew, vnew, bias_new, subln_g[l].reshape(1, V_DIM), seq_s=seq_s,
                                out_scale=out_scale)
        o_s = o_s.reshape(nb_s, N_KV_HEADS, REP, seq_s, V_DIM).transpose(0, 3, 1, 2, 4).reshape(n_s, N_HEADS * V_DIM)
        prev = state_conv[l]
        zeros = jnp.zeros((nb_s, seq_s - 1, cd), F32)
        fix1 = jnp.concatenate([prev[:, 1:2], zeros], axis=1).reshape(n_s, cd)
        fix2 = jnp.concatenate([prev, zeros[:, 1:]], axis=1).reshape(n_s, cd)
        part_s, ga_s, z_s, zv_s = _branch_proj(ys, w_br, row2(gmlp_ln_g[l]), row2(gmlp_ln_b[l]), ws_s,
                                               bs_s, conv_w[l], pg, pc, (fix1, fix2), prompt=False,
                                               tm=tm_s, seq_len=seq_s, period=seq_s)
        x1s = _mix_proj(ys, o_s, ga_s, part_s, pa, wo, row2(ln1_g[l]), row2(ln1_b[l]), tm=tm_s, alpha=alpha)
        ys = mixer(x1s, tm_s)
        outs["cs"].append(z_s.reshape(nb_s, seq_s, cd)[:, seq_s - (CONV_WIDTH - 1):])
        outs["gs"].append(zv_s.reshape(nb_s, seq_s, -1))

    return (yp.reshape(nb_p, seq_p, d_model), ys.reshape(nb_s, seq_s, d_model),
            kv_p[0].reshape(depth, nb_p, seq_p, N_KV_HEADS, QK_DIM),
            kv_p[1].reshape(depth, nb_p, seq_p, N_KV_HEADS, V_DIM), jnp.stack(outs["cp"]),
            kv_s[0].reshape(depth, nb_s, seq_s, N_KV_HEADS, QK_DIM),
            kv_s[1].reshape(depth, nb_s, seq_s, N_KV_HEADS, V_DIM),
            jnp.stack(outs["cs"]), jnp.stack(outs["gs"]))
```
